```python
import jax, jax.numpy as jnp
from jax import lax
import numpy as np

D_MODEL = 1024
BATCH = 8
SEQ = 2048
DEPTH = 4
DEC_BATCH = 128
DEC_SEQ = 8
PAST_LEN = 16384
PAGE_SIZE = 128

N_MIXERS = 2
N_A_LAYERS = (DEPTH + 1) // 2
N_B_LAYERS = DEPTH // 2
CHUNK = 128
A_HEADS = 4
A_WIDTH = 3 * D_MODEL
A_HEAD_DIM = A_WIDTH // A_HEADS
CONV_WIDTH = 3
B_WIDTH = D_MODEL
N_GROUPS = 4
EXPERTS_PER_GROUP = 4
N_EXPERTS = N_GROUPS * EXPERTS_PER_GROUP
TOP_K_INNER = 2
D_EXPERT = D_MODEL // 4
PLE_DIM = 256
LN_EPS = 1e-5
DEEPNORM_ALPHA = (2 * DEPTH) ** 0.25
DEEPNORM_BETA = (8 * DEPTH) ** -0.25

kernel_name = "hybrid_chunkgmlp_shortconv_hmoe_step"


def layer_norm(x, g, b):
    xf = x.astype(jnp.float32)
    mu = jnp.mean(xf, axis=-1, keepdims=True)
    var = jnp.mean(jnp.square(xf - mu), axis=-1, keepdims=True)
    y = (xf - mu) * lax.rsqrt(var + LN_EPS)
    return (y * g.astype(jnp.float32) + b.astype(jnp.float32)).astype(x.dtype)


def chunk_spatial_gate(v, w_s, b_s):
    bsz, s, _ = v.shape
    n_chunks = -(-s // CHUNK)
    pad = n_chunks * CHUNK - s
    vp = jnp.pad(v, ((0, 0), (0, pad), (0, 0))).reshape(bsz, n_chunks, CHUNK, A_HEADS, A_HEAD_DIM)
    causal = jnp.tril(jnp.ones((CHUNK, CHUNK), dtype=bool))
    w_masked = jnp.where(causal[None], w_s, jnp.zeros((), w_s.dtype))
    out = jnp.einsum("hts,bcshd->bcthd", w_masked, vp) + jnp.transpose(b_s)[:, :, None]
    return out.reshape(bsz, n_chunks * CHUNK, A_WIDTH)[:, :s]


def chunk_gmlp_mixer(x, w_in, b_in, ln_g, ln_b, w_s, b_s, w_out):
    z = jax.nn.gelu(jnp.einsum("bsd,de->bse", x, w_in) + b_in, approximate=False)
    u, v = jnp.split(z, 2, axis=-1)
    v = layer_norm(v, ln_g, ln_b)
    sg = chunk_spatial_gate(v, w_s, b_s)
    y = jnp.einsum("bse,ed->bsd", u * sg, w_out)
    return y, v


def short_conv_mixer(x, past, w_in, conv_w, w_out):
    s = x.shape[1]
    h = jnp.einsum("bsd,de->bse", x, w_in)
    gate_b, gate_c, hx = jnp.split(h, 3, axis=-1)
    hp = jnp.concatenate([past, gate_c * hx], axis=1)
    conv = conv_w[0] * hp[:, 0:s]
    for k in range(1, CONV_WIDTH):
        conv = conv + conv_w[k] * hp[:, k:k + s]
    y = jnp.einsum("bse,ed->bsd", gate_b * conv, w_out)
    return y, hp[:, s:]


def hier_moe(x, w_group, b_group, w_expert, b_expert, w_gate_up, w_down):
    bsz, s, d = x.shape
    xt = x.reshape(-1, d)
    g_logits = (xt @ w_group + b_group).astype(jnp.float32)
    g_probs = jax.nn.softmax(g_logits, axis=-1)
    g_idx = jnp.argmax(g_logits, axis=-1)
    g_w = jnp.take_along_axis(g_probs, g_idx[:, None], axis=-1)[:, 0]
    e_logits = (xt @ w_expert + b_expert).astype(jnp.float32).reshape(-1, N_GROUPS, EXPERTS_PER_GROUP)
    e_in_group = jnp.take_along_axis(e_logits, g_idx[:, None, None], axis=1)[:, 0]
    top_vals, top_idx = lax.top_k(e_in_group, TOP_K_INNER)
    top_w = jax.nn.softmax(top_vals, axis=-1) * g_w[:, None]
    expert_id = g_idx[:, None] * EXPERTS_PER_GROUP + top_idx
    combine = jnp.sum(jax.nn.one_hot(expert_id, N_EXPERTS, dtype=jnp.float32) * top_w[..., None], axis=1)
    hgu = jnp.einsum("td,edf->tef", xt, w_gate_up)
    gate, up = jnp.split(hgu, 2, axis=-1)
    a = jax.nn.silu(gate) * up * combine.astype(x.dtype)[:, :, None]
    y = jnp.einsum("tef,efd->td", a, w_down)
    return y.reshape(bsz, s, d)


def run_trunk(x, p, conv_past, a_w_in, a_b_in, a_ln_g, a_ln_b, a_w_s, a_b_s, a_w_out,
              b_w_in, b_conv_w, b_w_out, ln1_g, ln1_b, ln2_g, ln2_b,
              moe_w_group, moe_b_group, moe_w_expert, moe_b_expert, moe_w_gate_up, moe_w_down,
              ple_w_gate, ple_b_gate, ple_w_proj):
    v_states = []
    conv_states = []
    s = x.shape[1]
    open_start = ((s - 1) // CHUNK) * CHUNK
    for i in range(DEPTH):
        j = i // N_MIXERS
        if i % N_MIXERS == 0:
            mix, v = chunk_gmlp_mixer(x, a_w_in[j], a_b_in[j], a_ln_g[j], a_ln_b[j], a_w_s[j], a_b_s[j], a_w_out[j])
            v_states.append(v[:, open_start:])
        else:
            mix, buf = short_conv_mixer(x, conv_past[j], b_w_in[j], b_conv_w[j], b_w_out[j])
            conv_states.append(buf)
        x = layer_norm(DEEPNORM_ALPHA * x + mix, ln1_g[i], ln1_b[i])
        moe = hier_moe(x, moe_w_group[i], moe_b_group[i], moe_w_expert[i], moe_b_expert[i],
                       moe_w_gate_up[i], moe_w_down[i])
        x = layer_norm(DEEPNORM_ALPHA * x + moe, ln2_g[i], ln2_b[i])
        gate = jax.nn.sigmoid(jnp.einsum("bsd,de->bse", x, ple_w_gate[i]) + ple_b_gate[i])
        x = x + gate * jnp.einsum("bsp,pd->bsd", p[i], ple_w_proj[i])
    return x, jnp.stack(v_states), jnp.stack(conv_states)


def setup_inputs(seed: int = 0) -> dict:
    key = jax.random.key(seed)
    ks = jax.random.split(key, 32)
    f32 = jnp.float32

    def nrm(k, shape, scale):
        return jax.random.normal(k, shape, f32) * scale

    return {
        "x_prompt": nrm(ks[0], (BATCH, SEQ, D_MODEL), 1.0),
        "x_sample": nrm(ks[1], (DEC_BATCH, DEC_SEQ, D_MODEL), 1.0),
        "state_conv": nrm(ks[2], (N_B_LAYERS, DEC_BATCH, CONV_WIDTH - 1, B_WIDTH), 1.0),
        "p_prompt": nrm(ks[3], (DEPTH, BATCH, SEQ, PLE_DIM), 1.0),
        "p_sample": nrm(ks[4], (DEPTH, DEC_BATCH, DEC_SEQ, PLE_DIM), 1.0),
        "a_w_in": nrm(ks[5], (N_A_LAYERS, D_MODEL, 2 * A_WIDTH), D_MODEL ** -0.5),
        "a_b_in": nrm(ks[6], (N_A_LAYERS, 2 * A_WIDTH), 0.02),
        "a_ln_g": 1.0 + nrm(ks[7], (N_A_LAYERS, A_WIDTH), 0.05),
        "a_ln_b": nrm(ks[8], (N_A_LAYERS, A_WIDTH), 0.02),
        "a_w_s": nrm(ks[9], (N_A_LAYERS, A_HEADS, CHUNK, CHUNK), 0.5 * CHUNK ** -0.5),
        "a_b_s": 1.0 + nrm(ks[10], (N_A_LAYERS, A_HEADS, CHUNK), 0.1),
        "a_w_out": nrm(ks[11], (N_A_LAYERS, A_WIDTH, D_MODEL), DEEPNORM_BETA * A_WIDTH ** -0.5),
        "b_w_in": nrm(ks[12], (N_B_LAYERS, D_MODEL, 3 * B_WIDTH), D_MODEL ** -0.5),
        "b_conv_w": nrm(ks[13], (N_B_LAYERS, CONV_WIDTH, B_WIDTH), CONV_WIDTH ** -0.5),
        "b_w_out": nrm(ks[14], (N_B_LAYERS, B_WIDTH, D_MODEL), DEEPNORM_BETA * B_WIDTH ** -0.5),
        "ln1_g": 1.0 + nrm(ks[15], (DEPTH, D_MODEL), 0.05),
        "ln1_b": nrm(ks[16], (DEPTH, D_MODEL), 0.02),
        "ln2_g": 1.0 + nrm(ks[17], (DEPTH, D_MODEL), 0.05),
        "ln2_b": nrm(ks[18], (DEPTH, D_MODEL), 0.02),
        "moe_w_group": nrm(ks[19], (DEPTH, D_MODEL, N_GROUPS), D_MODEL ** -0.5),
        "moe_b_group": nrm(ks[20], (DEPTH, N_GROUPS), 0.01),
        "moe_w_expert": nrm(ks[21], (DEPTH, D_MODEL, N_EXPERTS), D_MODEL ** -0.5),
        "moe_b_expert": nrm(ks[22], (DEPTH, N_EXPERTS), 0.01),
        "moe_w_gate_up": nrm(ks[23], (DEPTH, N_EXPERTS, D_MODEL, 2 * D_EXPERT), D_MODEL ** -0.5),
        "moe_w_down": nrm(ks[24], (DEPTH, N_EXPERTS, D_EXPERT, D_MODEL), DEEPNORM_BETA * D_EXPERT ** -0.5),
        "ple_w_gate": nrm(ks[25], (DEPTH, D_MODEL, D_MODEL), D_MODEL ** -0.5),
        "ple_b_gate": nrm(ks[26], (DEPTH, D_MODEL), 0.02),
        "ple_w_proj": nrm(ks[27], (DEPTH, PLE_DIM, D_MODEL), DEEPNORM_BETA * PLE_DIM ** -0.5),
    }


def reference(x_prompt, x_sample, state_conv, p_prompt, p_sample,
              a_w_in, a_b_in, a_ln_g, a_ln_b, a_w_s, a_b_s, a_w_out,
              b_w_in, b_conv_w, b_w_out, ln1_g, ln1_b, ln2_g, ln2_b,
              moe_w_group, moe_b_group, moe_w_expert, moe_b_expert, moe_w_gate_up, moe_w_down,
              ple_w_gate, ple_b_gate, ple_w_proj):
    conv_zero = jnp.zeros((N_B_LAYERS, x_prompt.shape[0], CONV_WIDTH - 1, B_WIDTH), x_prompt.dtype)
    y_prompt, state_chunk_v_prompt, state_conv_prompt = run_trunk(
        x_prompt, p_prompt, conv_zero, a_w_in, a_b_in, a_ln_g, a_ln_b, a_w_s, a_b_s, a_w_out,
        b_w_in, b_conv_w, b_w_out, ln1_g, ln1_b, ln2_g, ln2_b,
        moe_w_group, moe_b_group, moe_w_expert, moe_b_expert, moe_w_gate_up, moe_w_down,
        ple_w_gate, ple_b_gate, ple_w_proj)
    y_sample, state_chunk_v_sample, state_conv_sample = run_trunk(
        x_sample, p_sample, state_conv, a_w_in, a_b_in, a_ln_g, a_ln_b, a_w_s, a_b_s, a_w_out,
        b_w_in, b_conv_w, b_w_out, ln1_g, ln1_b, ln2_g, ln2_b,
        moe_w_group, moe_b_group, moe_w_expert, moe_b_expert, moe_w_gate_up, moe_w_down,
        ple_w_gate, ple_b_gate, ple_w_proj)
    return (y_prompt, y_sample, state_chunk_v_prompt, state_chunk_v_sample, state_conv_prompt, state_conv_sample)
```

```python
import functools
import math

import jax
import jax.numpy as jnp
from jax import lax
from jax.experimental import pallas as pl
from jax.experimental.pallas import tpu as pltpu

D_MODEL = 1024
BATCH = 8
SEQ = 2048
DEPTH = 4
DEC_BATCH = 128
DEC_SEQ = 8
CHUNK = 128
A_HEADS = 4
A_WIDTH = 3 * D_MODEL
A_HEAD_DIM = A_WIDTH // A_HEADS
CONV_WIDTH = 3
B_WIDTH = D_MODEL
N_GROUPS = 4
EXPERTS_PER_GROUP = 4
N_EXPERTS = N_GROUPS * EXPERTS_PER_GROUP
D_EXPERT = D_MODEL // 4
PLE_DIM = 256
LN_EPS = 1e-5
DEEPNORM_ALPHA = (2 * DEPTH) ** 0.25

V7X_LANES = 128
V7X_SUBLANES = 8
V7X_VMEM_BYTES = 64 * 1024 * 1024

N_PROMPT_TOK = BATCH * SEQ
N_SAMPLE_TOK = DEC_BATCH * DEC_SEQ
N_TOK = N_PROMPT_TOK + N_SAMPLE_TOK
TM = 256
N_PROMPT_BLOCKS = N_PROMPT_TOK // TM
N_SAMPLE_BLOCKS = N_SAMPLE_TOK // TM
N_BLOCKS = N_PROMPT_BLOCKS + N_SAMPLE_BLOCKS
BLOCKS_PER_SEQ = SEQ // TM
N_STATE_BLOCKS = BATCH + N_SAMPLE_BLOCKS
ROUTER_LANES = V7X_LANES
VMEM_LIMIT = V7X_VMEM_BYTES - 8 * 1024 * 1024

assert SEQ % TM == 0 and N_SAMPLE_TOK % TM == 0 and TM % CHUNK == 0 and TM % DEC_SEQ == 0
assert CHUNK % DEC_SEQ == 0

_BF16 = jnp.bfloat16
_F32 = jnp.float32


def _layer_norm(x, g, b):
    mu = jnp.mean(x, axis=-1, keepdims=True)
    xc = x - mu
    var = jnp.mean(xc * xc, axis=-1, keepdims=True)
    return xc * lax.rsqrt(var + LN_EPS) * g + b


def _gelu(x):
    return 0.5 * x * (1.0 + lax.erf(x * math.sqrt(0.5)))


def _dot(a, b):
    return jnp.dot(a, b, preferred_element_type=_F32)


def _const_spec(shape):
    zeros = (0,) * len(shape)
    return pl.BlockSpec(shape, lambda i: zeros, pipeline_mode=pl.Buffered(1))


def _tok_spec(width):
    return pl.BlockSpec((TM, width), lambda i: (i, 0))


def _state_block(i):
    return jnp.where(i < N_PROMPT_BLOCKS, i // BLOCKS_PER_SEQ, i - N_PROMPT_BLOCKS + BATCH)


def _sample_block(i):
    return jnp.maximum(i - N_PROMPT_BLOCKS, 0)


_PARAMS = pltpu.CompilerParams(dimension_semantics=("arbitrary",), vmem_limit_bytes=VMEM_LIMIT)


def _mixer_a_kernel(x_ref, w_in_ref, b_in_ref, lng_ref, lnb_ref, ws_ref, bs_ref, w_out_ref,
                    g1_ref, b1_ref, x1_ref, v_ref, t_ref):
    i = pl.program_id(0)
    is_sample = i >= N_PROMPT_BLOCKS
    x = x_ref[...]
    xb = x.astype(_BF16)
    u = _gelu(_dot(xb, w_in_ref[:, :A_WIDTH]) + b_in_ref[:, :A_WIDTH])
    v = _gelu(_dot(xb, w_in_ref[:, A_WIDTH:]) + b_in_ref[:, A_WIDTH:])
    v = _layer_norm(v, lng_ref[...], lnb_ref[...])
    v_ref[...] = v
    vb = v.astype(_BF16)

    r = lax.broadcasted_iota(jnp.int32, (CHUNK, CHUNK), 0)
    c = lax.broadcasted_iota(jnp.int32, (CHUNK, CHUNK), 1)
    shift = jnp.where(is_sample, int(math.log2(DEC_SEQ)), int(math.log2(CHUNK)))
    mask = jnp.logical_and(r >= c, (r >> shift) == (c >> shift))
    for h in range(A_HEADS):
        w = jnp.where(mask, ws_ref[0, h], 0.0).astype(_BF16)
        bias = bs_ref[0, :, h:h + 1]
        cols = slice(h * A_HEAD_DIM, (h + 1) * A_HEAD_DIM)
        for k in range(TM // CHUNK):
            rows = slice(k * CHUNK, (k + 1) * CHUNK)
            sg = _dot(w, vb[rows, cols]) + bias
            t_ref[rows, cols] = (u[rows, cols] * sg).astype(_BF16)
    y = _dot(t_ref[...], w_out_ref[...])
    x1_ref[...] = _layer_norm(DEEPNORM_ALPHA * x + y, g1_ref[...], b1_ref[...])


def _mixer_a(x, w_in, b_in, ln_g, ln_b, ws2, bs2, w_out, g1, b1):
    return pl.pallas_call(
        _mixer_a_kernel,
        grid=(N_BLOCKS,),
        in_specs=[
            _tok_spec(D_MODEL),
            _const_spec((D_MODEL, 2 * A_WIDTH)),
            _const_spec((1, 2 * A_WIDTH)),
            _const_spec((1, A_WIDTH)),
            _const_spec((1, A_WIDTH)),
            pl.BlockSpec((1, A_HEADS, CHUNK, CHUNK), lambda i: (jnp.where(i < N_PROMPT_BLOCKS, 0, 1), 0, 0, 0)),
            pl.BlockSpec((1, CHUNK, V7X_LANES), lambda i: (jnp.where(i < N_PROMPT_BLOCKS, 0, 1), 0, 0)),
            _const_spec((A_WIDTH, D_MODEL)),
            _const_spec((1, D_MODEL)),
            _const_spec((1, D_MODEL)),
        ],
        out_specs=[
            _tok_spec(D_MODEL),
            pl.BlockSpec((TM, A_WIDTH), lambda i: (_state_block(i), 0)),
        ],
        out_shape=[
            jax.ShapeDtypeStruct((N_TOK, D_MODEL), _F32),
            jax.ShapeDtypeStruct((N_STATE_BLOCKS * TM, A_WIDTH), _F32),
        ],
        scratch_shapes=[pltpu.VMEM((TM, A_WIDTH), _BF16)],
        compiler_params=_PARAMS,
        name="mixer_a",
    )(x, w_in, b_in, ln_g, ln_b, ws2, bs2, w_out, g1, b1)


def _mixer_b_kernel(x_ref, w_in_ref, cw_ref, w_out_ref, g1_ref, b1_ref, past1_ref, past2_ref,
                    x1_ref, st_ref, ext_ref):
    i = pl.program_id(0)
    is_sample = i >= N_PROMPT_BLOCKS
    x = x_ref[...]
    h = _dot(x.astype(_BF16), w_in_ref[...])
    gate_b = h[:, :B_WIDTH]
    p = h[:, B_WIDTH:2 * B_WIDTH] * h[:, 2 * B_WIDTH:]
    st_ref[...] = p

    @pl.when(jnp.logical_or(is_sample, i % BLOCKS_PER_SEQ == 0))
    def _():
        ext_ref[0:V7X_SUBLANES, :] = jnp.zeros((V7X_SUBLANES, B_WIDTH), _F32)

    ext_ref[V7X_SUBLANES:, :] = p
    prev1 = ext_ref[V7X_SUBLANES - 1:V7X_SUBLANES - 1 + TM, :]
    prev2 = ext_ref[V7X_SUBLANES - 2:V7X_SUBLANES - 2 + TM, :]
    pos = lax.broadcasted_iota(jnp.int32, (TM, B_WIDTH), 0) % DEC_SEQ
    pos = jnp.where(is_sample, pos, DEC_SEQ)
    prev1 = jnp.where(pos < 1, past1_ref[...], prev1)
    prev2 = jnp.where(pos < 2, past2_ref[...], prev2)
    conv = cw_ref[0:1, :] * prev2 + cw_ref[1:2, :] * prev1 + cw_ref[2:3, :] * p
    y = _dot((gate_b * conv).astype(_BF16), w_out_ref[...])
    x1_ref[...] = _layer_norm(DEEPNORM_ALPHA * x + y, g1_ref[...], b1_ref[...])
    ext_ref[0:V7X_SUBLANES, :] = p[TM - V7X_SUBLANES:, :]


def _mixer_b(x, w_in, conv_w, w_out, g1, b1, past1, past2):
    return pl.pallas_call(
        _mixer_b_kernel,
        grid=(N_BLOCKS,),
        in_specs=[
            _tok_spec(D_MODEL),
            _const_spec((D_MODEL, 3 * B_WIDTH)),
            _const_spec((V7X_SUBLANES, B_WIDTH)),
            _const_spec((B_WIDTH, D_MODEL)),
            _const_spec((1, D_MODEL)),
            _const_spec((1, D_MODEL)),
            pl.BlockSpec((TM, B_WIDTH), lambda i: (_sample_block(i), 0)),
            pl.BlockSpec((TM, B_WIDTH), lambda i: (_sample_block(i), 0)),
        ],
        out_specs=[
            _tok_spec(D_MODEL),
            pl.BlockSpec((TM, B_WIDTH), lambda i: (_state_block(i), 0)),
        ],
        out_shape=[
            jax.ShapeDtypeStruct((N_TOK, D_MODEL), _F32),
            jax.ShapeDtypeStruct((N_STATE_BLOCKS * TM, B_WIDTH), _F32),
        ],
        scratch_shapes=[pltpu.VMEM((V7X_SUBLANES + TM, B_WIDTH), _F32)],
        compiler_params=_PARAMS,
        name="mixer_b",
    )(x, w_in, conv_w, w_out, g1, b1, past1, past2)


def _route(logits):
    lane = lax.broadcasted_iota(jnp.int32, logits.shape, 1)
    lane_f = lane.astype(_F32)
    neg = jnp.float32(-jnp.inf)
    far = jnp.float32(ROUTER_LANES)

    def first_argmax(vals):
        top = jnp.max(vals, axis=-1, keepdims=True)
        idx = jnp.min(jnp.where(vals == top, lane_f, far), axis=-1, keepdims=True)
        return top, idx

    is_group = lane < N_GROUPS
    g_top, g_idx = first_argmax(jnp.where(is_group, logits, neg))
    g_w = 1.0 / jnp.sum(jnp.where(is_group, jnp.exp(logits - g_top), 0.0), axis=-1, keepdims=True)
    lo = N_GROUPS + EXPERTS_PER_GROUP * g_idx
    in_group = jnp.logical_and(lane_f >= lo, lane_f < lo + EXPERTS_PER_GROUP)
    e_logits = jnp.where(in_group, logits, neg)
    v1, i1 = first_argmax(e_logits)
    v2, i2 = first_argmax(jnp.where(lane_f == i1, neg, e_logits))
    e2 = jnp.exp(v2 - v1)
    w1 = g_w / (1.0 + e2)
    w2 = g_w * e2 / (1.0 + e2)
    return jnp.where(lane_f == i1, w1, 0.0) + jnp.where(lane_f == i2, w2, 0.0)


def _moe_ple_kernel(x1_ref, wr_ref, br_ref, wgu_ref, wd_ref, g2_ref, b2_ref, wg_ref, bg_ref, wp_ref,
                    p_ref, out_ref, a_ref):
    x1 = x1_ref[...]
    xb = x1.astype(_BF16)
    combine = _route(_dot(xb, wr_ref[...]) + br_ref[...])
    for e in range(N_EXPERTS):
        hgu = _dot(xb, wgu_ref[e])
        gate = hgu[:, :D_EXPERT]
        up = hgu[:, D_EXPERT:]
        w = combine[:, N_GROUPS + e:N_GROUPS + e + 1]
        a_ref[:, e * D_EXPERT:(e + 1) * D_EXPERT] = (gate * jax.nn.sigmoid(gate) * up * w).astype(_BF16)
    moe = _dot(a_ref[...], wd_ref[...])
    x2 = _layer_norm(DEEPNORM_ALPHA * x1 + moe, g2_ref[...], b2_ref[...])
    gate = jax.nn.sigmoid(_dot(x2.astype(_BF16), wg_ref[...]) + bg_ref[...])
    out_ref[...] = x2 + gate * _dot(p_ref[...].astype(_BF16), wp_ref[...])


def _moe_ple(x1, wr, br, wgu, wd, g2, b2, wg, bg, wp, p):
    return pl.pallas_call(
        _moe_ple_kernel,
        grid=(N_BLOCKS,),
        in_specs=[
            _tok_spec(D_MODEL),
            _const_spec((D_MODEL, ROUTER_LANES)),
            _const_spec((1, ROUTER_LANES)),
            _const_spec((N_EXPERTS, D_MODEL, 2 * D_EXPERT)),
            _const_spec((N_EXPERTS * D_EXPERT, D_MODEL)),
            _const_spec((1, D_MODEL)),
            _const_spec((1, D_MODEL)),
            _const_spec((D_MODEL, D_MODEL)),
            _const_spec((1, D_MODEL)),
            _const_spec((PLE_DIM, D_MODEL)),
            _tok_spec(PLE_DIM),
        ],
        out_specs=_tok_spec(D_MODEL),
        out_shape=jax.ShapeDtypeStruct((N_TOK, D_MODEL), _F32),
        scratch_shapes=[pltpu.VMEM((TM, N_EXPERTS * D_EXPERT), _BF16)],
        compiler_params=_PARAMS,
        name="moe_ple",
    )(x1, wr, br, wgu, wd, g2, b2, wg, bg, wp, p)


def _row(v):
    return v.reshape(1, -1)


def _pad_lanes(a, lanes):
    return jnp.pad(a, [(0, 0)] * (a.ndim - 1) + [(0, lanes - a.shape[-1])])


def kernel(x_prompt, x_sample, state_conv, p_prompt, p_sample, a_w_in, a_b_in, a_ln_g, a_ln_b, a_w_s, a_b_s, a_w_out, b_w_in, b_conv_w, b_w_out, ln1_g, ln1_b, ln2_g, ln2_b, moe_w_group, moe_b_group, moe_w_expert, moe_b_expert, moe_w_gate_up, moe_w_down, ple_w_gate, ple_b_gate, ple_w_proj):
    x = jnp.concatenate([x_prompt.reshape(N_PROMPT_TOK, D_MODEL), x_sample.reshape(N_SAMPLE_TOK, D_MODEL)])
    p_all = jnp.concatenate([p_prompt.reshape(DEPTH, N_PROMPT_TOK, PLE_DIM),
                             p_sample.reshape(DEPTH, N_SAMPLE_TOK, PLE_DIM)], axis=1)

    reps = CHUNK // DEC_SEQ
    ws_sample = jnp.tile(a_w_s[:, :, :DEC_SEQ, :DEC_SEQ], (1, 1, reps, reps))
    ws2 = jnp.stack([a_w_s, ws_sample], axis=1)
    bs_sample = jnp.tile(a_b_s[:, :, :DEC_SEQ], (1, 1, reps))
    bs2 = _pad_lanes(jnp.swapaxes(jnp.stack([a_b_s, bs_sample], axis=1), -1, -2), V7X_LANES)

    zeros = jnp.zeros((state_conv.shape[0], DEC_BATCH, DEC_SEQ, B_WIDTH), _F32)
    past1 = zeros.at[:, :, 0].set(state_conv[:, :, 1]).reshape(-1, N_SAMPLE_TOK, B_WIDTH)
    past2 = (zeros.at[:, :, 0].set(state_conv[:, :, 0]).at[:, :, 1].set(state_conv[:, :, 1])
             .reshape(-1, N_SAMPLE_TOK, B_WIDTH))

    w_router = _pad_lanes(jnp.concatenate([moe_w_group, moe_w_expert], axis=-1), ROUTER_LANES).astype(_BF16)
    b_router = _pad_lanes(jnp.concatenate([moe_b_group, moe_b_expert], axis=-1), ROUTER_LANES)

    v_states, conv_states = [], []
    for i in range(DEPTH):
        j = i // 2
        if i % 2 == 0:
            x, v = _mixer_a(x, a_w_in[j].astype(_BF16), _row(a_b_in[j]), _row(a_ln_g[j]), _row(a_ln_b[j]),
                            ws2[j], bs2[j], a_w_out[j].astype(_BF16), _row(ln1_g[i]), _row(ln1_b[i]))
            v_states.append(v)
        else:
            x, st = _mixer_b(x, b_w_in[j].astype(_BF16), _pad_rows(b_conv_w[j]), b_w_out[j].astype(_BF16),
                             _row(ln1_g[i]), _row(ln1_b[i]), past1[j], past2[j])
            conv_states.append(st)
        x = _moe_ple(x, w_router[i], _row(b_router[i]), moe_w_gate_up[i].astype(_BF16),
                     moe_w_down[i].astype(_BF16).reshape(N_EXPERTS * D_EXPERT, D_MODEL),
                     _row(ln2_g[i]), _row(ln2_b[i]), ple_w_gate[i].astype(_BF16), _row(ple_b_gate[i]),
                     ple_w_proj[i].astype(_BF16), p_all[i])

    y_prompt = x[:N_PROMPT_TOK].reshape(BATCH, SEQ, D_MODEL)
    y_sample = x[N_PROMPT_TOK:].reshape(DEC_BATCH, DEC_SEQ, D_MODEL)
    v_all = jnp.stack(v_states)
    v_prompt = v_all[:, :BATCH * TM].reshape(-1, BATCH, TM, A_WIDTH)[:, :, TM - CHUNK:]
    v_sample = v_all[:, BATCH * TM:].reshape(-1, DEC_BATCH, DEC_SEQ, A_WIDTH)
    st_all = jnp.stack(conv_states)
    keep = CONV_WIDTH - 1
    st_prompt = st_all[:, :BATCH * TM].reshape(-1, BATCH, TM, B_WIDTH)[:, :, TM - keep:]
    st_sample = st_all[:, BATCH * TM:].reshape(-1, DEC_BATCH, DEC_SEQ, B_WIDTH)[:, :, DEC_SEQ - keep:]
    return (y_prompt, y_sample, v_prompt, v_sample, st_prompt, st_sample)


def _pad_rows(a):
    return jnp.pad(a, [(0, V7X_SUBLANES - a.shape[0]), (0, 0)])
```

```python
import functools
import math

import jax
import jax.numpy as jnp
from jax import lax
from jax.experimental import pallas as pl
from jax.experimental.pallas import tpu as pltpu

D_MODEL = 1024
BATCH = 8
SEQ = 2048
DEPTH = 4
DEC_BATCH = 128
DEC_SEQ = 8
CHUNK = 128
A_HEADS = 4
A_WIDTH = 3 * D_MODEL
A_HEAD_DIM = A_WIDTH // A_HEADS
CONV_WIDTH = 3
B_WIDTH = D_MODEL
N_GROUPS = 4
EXPERTS_PER_GROUP = 4
N_EXPERTS = N_GROUPS * EXPERTS_PER_GROUP
TOP_K = 2
D_EXPERT = D_MODEL // 4
PLE_DIM = 256
LN_EPS = 1e-5
DEEPNORM_ALPHA = (2 * DEPTH) ** 0.25

V7X_LANES = 128
V7X_SUBLANES = 8
V7X_VMEM_BYTES = 64 * 1024 * 1024

N_PROMPT_TOK = BATCH * SEQ
N_SAMPLE_TOK = DEC_BATCH * DEC_SEQ
N_TOK = N_PROMPT_TOK + N_SAMPLE_TOK
TM = 256
N_PROMPT_BLOCKS = N_PROMPT_TOK // TM
N_SAMPLE_BLOCKS = N_SAMPLE_TOK // TM
N_BLOCKS = N_PROMPT_BLOCKS + N_SAMPLE_BLOCKS
BLOCKS_PER_SEQ = SEQ // TM
N_STATE_BLOCKS = BATCH + N_SAMPLE_BLOCKS
ROUTER_LANES = V7X_LANES
VMEM_LIMIT = V7X_VMEM_BYTES - 8 * 1024 * 1024

TE = TM
MAX_TILES = (N_TOK * TOP_K) // TE + N_EXPERTS
SORTED_ROWS = MAX_TILES * TE
ISSUE_UNROLL = 8

assert SEQ % TM == 0 and N_SAMPLE_TOK % TM == 0 and TM % CHUNK == 0 and TM % DEC_SEQ == 0
assert CHUNK % DEC_SEQ == 0 and TM % ISSUE_UNROLL == 0 and (N_TOK * TOP_K) % TE == 0

_BF16 = jnp.bfloat16
_F32 = jnp.float32
_I32 = jnp.int32


def _layer_norm(x, g, b):
    mu = jnp.mean(x, axis=-1, keepdims=True)
    xc = x - mu
    var = jnp.mean(xc * xc, axis=-1, keepdims=True)
    return xc * lax.rsqrt(var + LN_EPS) * g + b


def _gelu(x):
    return 0.5 * x * (1.0 + lax.erf(x * math.sqrt(0.5)))


def _dot(a, b):
    return jnp.dot(a, b, preferred_element_type=_F32)


def _const_spec(shape):
    zeros = (0,) * len(shape)
    return pl.BlockSpec(shape, lambda i, *_: zeros, pipeline_mode=pl.Buffered(1))


def _tok_spec(width):
    return pl.BlockSpec((TM, width), lambda i, *_: (i, 0))


def _state_block(i):
    return jnp.where(i < N_PROMPT_BLOCKS, i // BLOCKS_PER_SEQ, i - N_PROMPT_BLOCKS + BATCH)


def _sample_block(i):
    return jnp.maximum(i - N_PROMPT_BLOCKS, 0)


_PARAMS = pltpu.CompilerParams(dimension_semantics=("arbitrary",), vmem_limit_bytes=VMEM_LIMIT)


def _route(logits):
    lane = lax.broadcasted_iota(_I32, logits.shape, 1)
    lane_f = lane.astype(_F32)
    neg = jnp.float32(-jnp.inf)
    far = jnp.float32(ROUTER_LANES)

    def first_argmax(vals):
        top = jnp.max(vals, axis=-1, keepdims=True)
        idx = jnp.min(jnp.where(vals == top, lane_f, far), axis=-1, keepdims=True)
        return top, idx

    is_group = lane < N_GROUPS
    g_top, g_idx = first_argmax(jnp.where(is_group, logits, neg))
    g_w = 1.0 / jnp.sum(jnp.where(is_group, jnp.exp(logits - g_top), 0.0), axis=-1, keepdims=True)
    lo = N_GROUPS + EXPERTS_PER_GROUP * g_idx
    in_group = jnp.logical_and(lane_f >= lo, lane_f < lo + EXPERTS_PER_GROUP)
    e_logits = jnp.where(in_group, logits, neg)
    v1, i1 = first_argmax(e_logits)
    v2, i2 = first_argmax(jnp.where(lane_f == i1, neg, e_logits))
    e2 = jnp.exp(v2 - v1)
    return i1, i2, g_w / (1.0 + e2), g_w * e2 / (1.0 + e2)


_DISPATCH_IN_SPECS = [_const_spec((D_MODEL, ROUTER_LANES)), _const_spec((1, ROUTER_LANES))]
_DISPATCH_OUT_SPECS = [
    pl.BlockSpec((V7X_SUBLANES, TM), lambda i: (i, 0)),
    pl.BlockSpec((TM, V7X_LANES), lambda i: (i, 0)),
    pl.BlockSpec((V7X_SUBLANES, V7X_LANES), lambda i: (i, 0)),
    pl.BlockSpec(memory_space=pl.ANY),
]
_DISPATCH_OUT_SHAPES = [
    jax.ShapeDtypeStruct((N_BLOCKS * V7X_SUBLANES, TM), _I32),
    jax.ShapeDtypeStruct((N_TOK, V7X_LANES), _F32),
    jax.ShapeDtypeStruct((N_BLOCKS * V7X_SUBLANES, V7X_LANES), _F32),
    jax.ShapeDtypeStruct((SORTED_ROWS, D_MODEL), _F32),
]
_DISPATCH_SCRATCH = [
    pltpu.VMEM((TM, D_MODEL), _F32),
    pltpu.VMEM((V7X_SUBLANES, TM), _I32),
    pltpu.SMEM((V7X_SUBLANES, TM), _I32),
    pltpu.VMEM((V7X_SUBLANES, V7X_LANES), _F32),
    pltpu.VMEM((V7X_SUBLANES, V7X_LANES), _I32),
    pltpu.SMEM((V7X_SUBLANES, V7X_LANES), _I32),
    pltpu.SemaphoreType.DMA((3,)),
]
_N_DISPATCH_OUT = len(_DISPATCH_OUT_SPECS)
_N_DISPATCH_SCRATCH = len(_DISPATCH_SCRATCH)
_INV_TE = 1.0 / TE


def _tiles_for(rows):
    return jnp.floor((rows + (TE - 1)) * _INV_TE)


def _dispatch(x1, wr_ref, br_ref, pos_ref, wc_ref, cnt_ref, xs_ref,
              rows_ref, pos_v, pos_s, state_ref, cnt_v, cnt_s, sems):
    i = pl.program_id(0)
    last = pl.num_programs(0) - 1
    logits = _dot(x1.astype(_BF16), wr_ref[...]) + br_ref[...]
    i1, i2, w1, w2 = _route(logits)
    lane = lax.broadcasted_iota(_I32, (TM, ROUTER_LANES), 1)
    lane_f = lane.astype(_F32)
    hit1 = lane_f == i1
    hit2 = lane_f == i2
    onehot = jnp.where(jnp.logical_or(hit1, hit2), 1.0, 0.0)

    @pl.when(i == 0)
    def _():
        state_ref[...] = jnp.zeros_like(state_ref)

    count = state_ref[0:1, :]
    current = state_ref[1:2, :]
    handed_out = state_ref[2:3, :]
    new_count = count + jnp.sum(onehot, axis=0, keepdims=True)
    full_before = _tiles_for(count)
    opens_tile = _tiles_for(new_count) - full_before
    lr = lax.broadcasted_iota(_I32, (ROUTER_LANES, ROUTER_LANES), 0)
    lc = lax.broadcasted_iota(_I32, (ROUTER_LANES, ROUTER_LANES), 1)
    lower_lanes = jnp.where(lr < lc, 1.0, 0.0).astype(_BF16)
    opened_before = _dot(jnp.broadcast_to(opens_tile, (V7X_SUBLANES, ROUTER_LANES)).astype(_BF16), lower_lanes)
    fresh = handed_out + opened_before[0:1, :]

    r = lax.broadcasted_iota(_I32, (TM, TM), 0)
    c = lax.broadcasted_iota(_I32, (TM, TM), 1)
    earlier = jnp.where(r > c, 1.0, 0.0).astype(_BF16)
    taken = _dot(earlier, onehot.astype(_BF16)) + count
    tile = jnp.where(taken < full_before * TE, current, fresh)
    row = tile * TE + (taken - jnp.floor(taken * _INV_TE) * TE)
    pos1 = jnp.sum(jnp.where(hit1, row, 0.0), axis=-1, keepdims=True)
    pos2 = jnp.sum(jnp.where(hit2, row, 0.0), axis=-1, keepdims=True)
    state_ref[0:1, :] = new_count
    state_ref[1:2, :] = jnp.where(opens_tile > 0.0, fresh, current)
    state_ref[2:3, :] = handed_out + jnp.sum(opens_tile, axis=-1, keepdims=True)
    cnt_ref[...] = jnp.broadcast_to(new_count, cnt_ref.shape)
    wc_ref[...] = jnp.where(lane == 0, w1, jnp.where(lane == 1, w2, 0.0))
    pos_rows = jnp.where(lane == 0, pos1, jnp.where(lane == 1, pos2, 0.0)).T[0:V7X_SUBLANES, :].astype(_I32)
    pos_ref[...] = pos_rows

    def wait_block_rows():
        for _ in range(TOP_K):
            pltpu.make_async_copy(rows_ref, xs_ref.at[pl.ds(0, TM)], sems.at[0]).wait()

    @pl.when(i > 0)
    def _():
        wait_block_rows()

    rows_ref[...] = x1
    pos_v[...] = pos_rows
    to_smem = pltpu.make_async_copy(pos_v, pos_s, sems.at[2])
    to_smem.start()
    to_smem.wait()

    def issue(t, carry):
        src = rows_ref.at[pl.ds(t, 1)]
        for k in range(TOP_K):
            pltpu.make_async_copy(src, xs_ref.at[pl.ds(pos_s[k, t], 1)], sems.at[0]).start()
        return carry

    lax.fori_loop(0, TM, issue, 0, unroll=ISSUE_UNROLL)

    @pl.when(i == last)
    def _():
        wait_block_rows()
        rows_ref[...] = jnp.zeros_like(rows_ref)
        cnt_v[...] = state_ref[...].astype(_I32)
        state_to_smem = pltpu.make_async_copy(cnt_v, cnt_s, sems.at[2])
        state_to_smem.start()
        state_to_smem.wait()
        for e in range(N_EXPERTS):
            used = cnt_s[0, N_GROUPS + e] % TE
            pad = (TE - used) % TE
            first = cnt_s[1, N_GROUPS + e] * TE + used

            def fill(t, carry):
                pltpu.make_async_copy(rows_ref.at[pl.ds(0, 1)], xs_ref.at[pl.ds(first + t, 1)], sems.at[1]).start()
                return carry

            def drain(t, carry):
                pltpu.make_async_copy(rows_ref.at[pl.ds(0, 1)], xs_ref.at[pl.ds(first + t, 1)], sems.at[1]).wait()
                return carry

            lax.fori_loop(0, pad, fill, 0)
            lax.fori_loop(0, pad, drain, 0)

        def tile_copy(k):
            return pltpu.make_async_copy(rows_ref, xs_ref.at[pl.ds(pl.multiple_of(k * TE, TE), TE)], sems.at[1])

        def fill_tile(k, carry):
            tile_copy(k).start()
            return carry

        def drain_tile(k, carry):
            tile_copy(k).wait()
            return carry

        lax.fori_loop(cnt_s[2, 0], MAX_TILES, fill_tile, 0)
        lax.fori_loop(cnt_s[2, 0], MAX_TILES, drain_tile, 0)


def _mixer_a_kernel(x_ref, w_in_ref, b_in_ref, lng_ref, lnb_ref, ws_ref, bs_ref, w_out_ref,
                    g1_ref, b1_ref, wr_ref, br_ref, x1_ref, v_ref, *rest):
    dispatch_refs = rest[:_N_DISPATCH_OUT]
    t_ref = rest[_N_DISPATCH_OUT]
    dispatch_scratch = rest[_N_DISPATCH_OUT + 1:]
    i = pl.program_id(0)
    is_sample = i >= N_PROMPT_BLOCKS
    x = x_ref[...]
    xb = x.astype(_BF16)
    u = _gelu(_dot(xb, w_in_ref[:, :A_WIDTH]) + b_in_ref[:, :A_WIDTH])
    v = _gelu(_dot(xb, w_in_ref[:, A_WIDTH:]) + b_in_ref[:, A_WIDTH:])
    v = _layer_norm(v, lng_ref[...], lnb_ref[...])
    v_ref[...] = v
    vb = v.astype(_BF16)

    r = lax.broadcasted_iota(_I32, (CHUNK, CHUNK), 0)
    c = lax.broadcasted_iota(_I32, (CHUNK, CHUNK), 1)
    shift = jnp.where(is_sample, int(math.log2(DEC_SEQ)), int(math.log2(CHUNK)))
    mask = jnp.logical_and(r >= c, (r >> shift) == (c >> shift))
    for h in range(A_HEADS):
        w = jnp.where(mask, ws_ref[0, h], 0.0).astype(_BF16)
        bias = bs_ref[0, :, h:h + 1]
        cols = slice(h * A_HEAD_DIM, (h + 1) * A_HEAD_DIM)
        for k in range(TM // CHUNK):
            rows = slice(k * CHUNK, (k + 1) * CHUNK)
            sg = _dot(w, vb[rows, cols]) + bias
            t_ref[rows, cols] = (u[rows, cols] * sg).astype(_BF16)
    y = _dot(t_ref[...], w_out_ref[...])
    x1 = _layer_norm(DEEPNORM_ALPHA * x + y, g1_ref[...], b1_ref[...])
    x1_ref[...] = x1
    _dispatch(x1, wr_ref, br_ref, *dispatch_refs, *dispatch_scratch)


def _mixer_a(x, w_in, b_in, ln_g, ln_b, ws2, bs2, w_out, g1, b1, wr, br):
    return pl.pallas_call(
        _mixer_a_kernel,
        grid=(N_BLOCKS,),
        in_specs=[
            _tok_spec(D_MODEL),
            _const_spec((D_MODEL, 2 * A_WIDTH)),
            _const_spec((1, 2 * A_WIDTH)),
            _const_spec((1, A_WIDTH)),
            _const_spec((1, A_WIDTH)),
            pl.BlockSpec((1, A_HEADS, CHUNK, CHUNK), lambda i: (jnp.where(i < N_PROMPT_BLOCKS, 0, 1), 0, 0, 0)),
            pl.BlockSpec((1, CHUNK, V7X_LANES), lambda i: (jnp.where(i < N_PROMPT_BLOCKS, 0, 1), 0, 0)),
            _const_spec((A_WIDTH, D_MODEL)),
            _const_spec((1, D_MODEL)),
            _const_spec((1, D_MODEL)),
        ] + _DISPATCH_IN_SPECS,
        out_specs=[
            _tok_spec(D_MODEL),
            pl.BlockSpec((TM, A_WIDTH), lambda i: (_state_block(i), 0)),
        ] + _DISPATCH_OUT_SPECS,
        out_shape=[
            jax.ShapeDtypeStruct((N_TOK, D_MODEL), _F32),
            jax.ShapeDtypeStruct((N_STATE_BLOCKS * TM, A_WIDTH), _F32),
        ] + _DISPATCH_OUT_SHAPES,
        scratch_shapes=[pltpu.VMEM((TM, A_WIDTH), _BF16)] + _DISPATCH_SCRATCH,
        compiler_params=_PARAMS,
        name="mixer_a",
    )(x, w_in, b_in, ln_g, ln_b, ws2, bs2, w_out, g1, b1, wr, br)


def _mixer_b_kernel(x_ref, w_in_ref, cw_ref, w_out_ref, g1_ref, b1_ref, past1_ref, past2_ref,
                    wr_ref, br_ref, x1_ref, st_ref, *rest):
    dispatch_refs = rest[:_N_DISPATCH_OUT]
    ext_ref = rest[_N_DISPATCH_OUT]
    dispatch_scratch = rest[_N_DISPATCH_OUT + 1:]
    i = pl.program_id(0)
    is_sample = i >= N_PROMPT_BLOCKS
    x = x_ref[...]
    h = _dot(x.astype(_BF16), w_in_ref[...])
    gate_b = h[:, :B_WIDTH]
    p = h[:, B_WIDTH:2 * B_WIDTH] * h[:, 2 * B_WIDTH:]
    st_ref[...] = p

    @pl.when(jnp.logical_or(is_sample, i % BLOCKS_PER_SEQ == 0))
    def _():
        ext_ref[0:V7X_SUBLANES, :] = jnp.zeros((V7X_SUBLANES, B_WIDTH), _F32)

    ext_ref[V7X_SUBLANES:, :] = p
    prev1 = ext_ref[V7X_SUBLANES - 1:V7X_SUBLANES - 1 + TM, :]
    prev2 = ext_ref[V7X_SUBLANES - 2:V7X_SUBLANES - 2 + TM, :]
    pos = lax.broadcasted_iota(_I32, (TM, B_WIDTH), 0) % DEC_SEQ
    pos = jnp.where(is_sample, pos, DEC_SEQ)
    prev1 = jnp.where(pos < 1, past1_ref[...], prev1)
    prev2 = jnp.where(pos < 2, past2_ref[...], prev2)
    conv = cw_ref[0:1, :] * prev2 + cw_ref[1:2, :] * prev1 + cw_ref[2:3, :] * p
    y = _dot((gate_b * conv).astype(_BF16), w_out_ref[...])
    x1 = _layer_norm(DEEPNORM_ALPHA * x + y, g1_ref[...], b1_ref[...])
    x1_ref[...] = x1
    ext_ref[0:V7X_SUBLANES, :] = p[TM - V7X_SUBLANES:, :]
    _dispatch(x1, wr_ref, br_ref, *dispatch_refs, *dispatch_scratch)


def _mixer_b(x, w_in, conv_w, w_out, g1, b1, past1, past2, wr, br):
    return pl.pallas_call(
        _mixer_b_kernel,
        grid=(N_BLOCKS,),
        in_specs=[
            _tok_spec(D_MODEL),
            _const_spec((D_MODEL, 3 * B_WIDTH)),
            _const_spec((V7X_SUBLANES, B_WIDTH)),
            _const_spec((B_WIDTH, D_MODEL)),
            _const_spec((1, D_MODEL)),
            _const_spec((1, D_MODEL)),
            pl.BlockSpec((TM, B_WIDTH), lambda i: (_sample_block(i), 0)),
            pl.BlockSpec((TM, B_WIDTH), lambda i: (_sample_block(i), 0)),
        ] + _DISPATCH_IN_SPECS,
        out_specs=[
            _tok_spec(D_MODEL),
            pl.BlockSpec((TM, B_WIDTH), lambda i: (_state_block(i), 0)),
        ] + _DISPATCH_OUT_SPECS,
        out_shape=[
            jax.ShapeDtypeStruct((N_TOK, D_MODEL), _F32),
            jax.ShapeDtypeStruct((N_STATE_BLOCKS * TM, B_WIDTH), _F32),
        ] + _DISPATCH_OUT_SHAPES,
        scratch_shapes=[pltpu.VMEM((V7X_SUBLANES + TM, B_WIDTH), _F32)] + _DISPATCH_SCRATCH,
        compiler_params=_PARAMS,
        name="mixer_b",
    )(x, w_in, conv_w, w_out, g1, b1, past1, past2, wr, br)


def _experts_kernel(tile_expert, tile_used, xs_ref, wgu_ref, wd_ref, ys_ref):
    del tile_expert

    @pl.when(tile_used[pl.program_id(0)] == 1)
    def _():
        hgu = _dot(xs_ref[...].astype(_BF16), wgu_ref[0])
        gate = hgu[:, :D_EXPERT]
        up = hgu[:, D_EXPERT:]
        ys_ref[...] = _dot((gate * jax.nn.sigmoid(gate) * up).astype(_BF16), wd_ref[0])

    @pl.when(tile_used[pl.program_id(0)] == 0)
    def _():
        ys_ref[...] = jnp.zeros_like(ys_ref)


def _experts(tile_expert, tile_used, xs, wgu, wd):
    grid_spec = pltpu.PrefetchScalarGridSpec(
        num_scalar_prefetch=2,
        grid=(MAX_TILES,),
        in_specs=[
            pl.BlockSpec((TE, D_MODEL), lambda i, te, tu: (i, 0)),
            pl.BlockSpec((1, D_MODEL, 2 * D_EXPERT), lambda i, te, tu: (te[i], 0, 0)),
            pl.BlockSpec((1, D_EXPERT, D_MODEL), lambda i, te, tu: (te[i], 0, 0)),
        ],
        out_specs=pl.BlockSpec((TE, D_MODEL), lambda i, te, tu: (i, 0)),
    )
    return pl.pallas_call(
        _experts_kernel,
        grid_spec=grid_spec,
        out_shape=jax.ShapeDtypeStruct((SORTED_ROWS, D_MODEL), _F32),
        compiler_params=_PARAMS,
        name="experts",
    )(tile_expert, tile_used, xs, wgu, wd)


def _tile_owners(counts_after_block):
    tiles_after = (counts_after_block + (TE - 1)) // TE
    opened = jnp.diff(tiles_after, axis=0, prepend=0).reshape(-1)
    handed_out = jnp.cumsum(opened)
    total = handed_out[-1]
    t = jnp.arange(MAX_TILES, dtype=_I32)
    first_slot = jnp.sum((handed_out[None, :] <= jnp.minimum(t, total - 1)[:, None]).astype(_I32), axis=1)
    return (first_slot % N_EXPERTS).astype(_I32), (t < total).astype(_I32)


def _combine_ple_kernel(pos_ref, x1_ref, wc_ref, ys_ref, g2_ref, b2_ref, wg_ref, bg_ref, wp_ref, p_ref,
                        out_ref, buf_ref, sems):
    i = pl.program_id(0)
    slot = i % 2

    def issue(block, dst_slot):
        def body(t, carry):
            for k in range(TOP_K):
                row = pos_ref[(block * TOP_K + k) * TM + t]
                pltpu.make_async_copy(ys_ref.at[pl.ds(row, 1)], buf_ref.at[dst_slot, k, pl.ds(t, 1)],
                                      sems.at[dst_slot]).start()
            return carry

        lax.fori_loop(0, TM, body, 0, unroll=ISSUE_UNROLL)

    @pl.when(i == 0)
    def _():
        issue(0, 0)

    @pl.when(i + 1 < pl.num_programs(0))
    def _():
        issue(i + 1, 1 - slot)

    for k in range(TOP_K):
        pltpu.make_async_copy(ys_ref.at[pl.ds(0, TM)], buf_ref.at[slot, k], sems.at[slot]).wait()

    x1 = x1_ref[...]
    moe = wc_ref[:, 0:1] * buf_ref[slot, 0] + wc_ref[:, 1:2] * buf_ref[slot, 1]
    x2 = _layer_norm(DEEPNORM_ALPHA * x1 + moe, g2_ref[...], b2_ref[...])
    gate = jax.nn.sigmoid(_dot(x2.astype(_BF16), wg_ref[...]) + bg_ref[...])
    out_ref[...] = x2 + gate * _dot(p_ref[...].astype(_BF16), wp_ref[...])


def _combine_ple(pos, x1, wc, ys, g2, b2, wg, bg, wp, p):
    grid_spec = pltpu.PrefetchScalarGridSpec(
        num_scalar_prefetch=1,
        grid=(N_BLOCKS,),
        in_specs=[
            _tok_spec(D_MODEL),
            _tok_spec(V7X_LANES),
            pl.BlockSpec(memory_space=pl.ANY),
            _const_spec((1, D_MODEL)),
            _const_spec((1, D_MODEL)),
            _const_spec((D_MODEL, D_MODEL)),
            _const_spec((1, D_MODEL)),
            _const_spec((PLE_DIM, D_MODEL)),
            _tok_spec(PLE_DIM),
        ],
        out_specs=_tok_spec(D_MODEL),
        scratch_shapes=[pltpu.VMEM((2, TOP_K, TM, D_MODEL), _F32), pltpu.SemaphoreType.DMA((2,))],
    )
    return pl.pallas_call(
        _combine_ple_kernel,
        grid_spec=grid_spec,
        out_shape=jax.ShapeDtypeStruct((N_TOK, D_MODEL), _F32),
        compiler_params=_PARAMS,
        name="combine_ple",
    )(pos, x1, wc, ys, g2, b2, wg, bg, wp, p)


def _row(v):
    return v.reshape(1, -1)


def _pad_lanes(a, lanes):
    return jnp.pad(a, [(0, 0)] * (a.ndim - 1) + [(0, lanes - a.shape[-1])])


def _pad_rows(a):
    return jnp.pad(a, [(0, V7X_SUBLANES - a.shape[0]), (0, 0)])


def kernel(x_prompt, x_sample, state_conv, p_prompt, p_sample, a_w_in, a_b_in, a_ln_g, a_ln_b, a_w_s, a_b_s, a_w_out, b_w_in, b_conv_w, b_w_out, ln1_g, ln1_b, ln2_g, ln2_b, moe_w_group, moe_b_group, moe_w_expert, moe_b_expert, moe_w_gate_up, moe_w_down, ple_w_gate, ple_b_gate, ple_w_proj):
    x = jnp.concatenate([x_prompt.reshape(N_PROMPT_TOK, D_MODEL), x_sample.reshape(N_SAMPLE_TOK, D_MODEL)])
    p_all = jnp.concatenate([p_prompt.reshape(DEPTH, N_PROMPT_TOK, PLE_DIM),
                             p_sample.reshape(DEPTH, N_SAMPLE_TOK, PLE_DIM)], axis=1)

    reps = CHUNK // DEC_SEQ
    ws_sample = jnp.tile(a_w_s[:, :, :DEC_SEQ, :DEC_SEQ], (1, 1, reps, reps))
    ws2 = jnp.stack([a_w_s, ws_sample], axis=1)
    bs_sample = jnp.tile(a_b_s[:, :, :DEC_SEQ], (1, 1, reps))
    bs2 = _pad_lanes(jnp.swapaxes(jnp.stack([a_b_s, bs_sample], axis=1), -1, -2), V7X_LANES)

    zeros = jnp.zeros((state_conv.shape[0], DEC_BATCH, DEC_SEQ, B_WIDTH), _F32)
    past1 = zeros.at[:, :, 0].set(state_conv[:, :, 1]).reshape(-1, N_SAMPLE_TOK, B_WIDTH)
    past2 = (zeros.at[:, :, 0].set(state_conv[:, :, 0]).at[:, :, 1].set(state_conv[:, :, 1])
             .reshape(-1, N_SAMPLE_TOK, B_WIDTH))

    w_router = _pad_lanes(jnp.concatenate([moe_w_group, moe_w_expert], axis=-1), ROUTER_LANES).astype(_BF16)
    b_router = _pad_lanes(jnp.concatenate([moe_b_group, moe_b_expert], axis=-1), ROUTER_LANES)

    v_states, conv_states = [], []
    for i in range(DEPTH):
        j = i // 2
        router = (w_router[i], _row(b_router[i]))
        if i % 2 == 0:
            x1, v, pos, wc, cnt, xs = _mixer_a(
                x, a_w_in[j].astype(_BF16), _row(a_b_in[j]), _row(a_ln_g[j]), _row(a_ln_b[j]),
                ws2[j], bs2[j], a_w_out[j].astype(_BF16), _row(ln1_g[i]), _row(ln1_b[i]), *router)
            v_states.append(v)
        else:
            x1, st, pos, wc, cnt, xs = _mixer_b(
                x, b_w_in[j].astype(_BF16), _pad_rows(b_conv_w[j]), b_w_out[j].astype(_BF16),
                _row(ln1_g[i]), _row(ln1_b[i]), past1[j], past2[j], *router)
            conv_states.append(st)
        counts = cnt.reshape(N_BLOCKS, V7X_SUBLANES, V7X_LANES)[:, 0, N_GROUPS:N_GROUPS + N_EXPERTS].astype(_I32)
        ys = _experts(*_tile_owners(counts), xs, moe_w_gate_up[i].astype(_BF16), moe_w_down[i].astype(_BF16))
        pos_flat = pos.reshape(N_BLOCKS, V7X_SUBLANES, TM)[:, :TOP_K].reshape(-1)
        x = _combine_ple(pos_flat, x1, wc, ys, _row(ln2_g[i]), _row(ln2_b[i]), ple_w_gate[i].astype(_BF16),
                         _row(ple_b_gate[i]), ple_w_proj[i].astype(_BF16), p_all[i])

    y_prompt = x[:N_PROMPT_TOK].reshape(BATCH, SEQ, D_MODEL)
    y_sample = x[N_PROMPT_TOK:].reshape(DEC_BATCH, DEC_SEQ, D_MODEL)
    v_all = jnp.stack(v_states)
    v_prompt = v_all[:, :BATCH * TM].reshape(-1, BATCH, TM, A_WIDTH)[:, :, TM - CHUNK:]
    v_sample = v_all[:, BATCH * TM:].reshape(-1, DEC_BATCH, DEC_SEQ, A_WIDTH)
    st_all = jnp.stack(conv_states)
    keep = CONV_WIDTH - 1
    st_prompt = st_all[:, :BATCH * TM].reshape(-1, BATCH, TM, B_WIDTH)[:, :, TM - keep:]
    st_sample = st_all[:, BATCH * TM:].reshape(-1, DEC_BATCH, DEC_SEQ, B_WIDTH)[:, :, DEC_SEQ - keep:]
    return (y_prompt, y_sample, v_prompt, v_sample, st_prompt, st_sample)
```

```python
import functools
import math

import jax
import jax.numpy as jnp
from jax import lax
from jax.experimental import pallas as pl
from jax.experimental.pallas import tpu as pltpu

D_MODEL = 1024
BATCH = 8
SEQ = 2048
DEPTH = 4
DEC_BATCH = 128
DEC_SEQ = 8
CHUNK = 128
A_HEADS = 4
A_WIDTH = 3 * D_MODEL
A_HEAD_DIM = A_WIDTH // A_HEADS
CONV_WIDTH = 3
B_WIDTH = D_MODEL
N_GROUPS = 4
EXPERTS_PER_GROUP = 4
N_EXPERTS = N_GROUPS * EXPERTS_PER_GROUP
D_EXPERT = D_MODEL // 4
PLE_DIM = 256
LN_EPS = 1e-5
DEEPNORM_ALPHA = (2 * DEPTH) ** 0.25

V7X_LANES = 128
V7X_SUBLANES = 8
V7X_VMEM_BYTES = 64 * 1024 * 1024

N_PROMPT_TOK = BATCH * SEQ
N_SAMPLE_TOK = DEC_BATCH * DEC_SEQ
N_TOK = N_PROMPT_TOK + N_SAMPLE_TOK
TM = 256
N_PROMPT_BLOCKS = N_PROMPT_TOK // TM
N_SAMPLE_BLOCKS = N_SAMPLE_TOK // TM
N_BLOCKS = N_PROMPT_BLOCKS + N_SAMPLE_BLOCKS
BLOCKS_PER_SEQ = SEQ // TM
ROUTER_LANES = V7X_LANES
VMEM_LIMIT = V7X_VMEM_BYTES - 8 * 1024 * 1024

assert SEQ % TM == 0 and N_SAMPLE_TOK % TM == 0 and TM % CHUNK == 0 and TM % DEC_SEQ == 0
assert CHUNK % DEC_SEQ == 0 and CONV_WIDTH - 1 <= V7X_SUBLANES

_BF16 = jnp.bfloat16
_F32 = jnp.float32
_I32 = jnp.int32


def _layer_norm(x, g, b):
    mu = jnp.mean(x, axis=-1, keepdims=True)
    xc = x - mu
    var = jnp.mean(xc * xc, axis=-1, keepdims=True)
    return xc * lax.rsqrt(var + LN_EPS) * g + b


def _gelu(x):
    return 0.5 * x * (1.0 + lax.erf(x * math.sqrt(0.5)))


def _dot(a, b):
    return jnp.dot(a, b, preferred_element_type=_F32)


def _const_spec(shape):
    zeros = (0,) * len(shape)
    return pl.BlockSpec(shape, lambda i: zeros, pipeline_mode=pl.Buffered(1))


def _tok_spec(width):
    return pl.BlockSpec((TM, width), lambda i: (i, 0))


def _prompt_block(i):
    return jnp.minimum(i, N_PROMPT_BLOCKS - 1)


def _sample_block(i):
    return jnp.maximum(i - N_PROMPT_BLOCKS, 0)


def _prompt_spec(width):
    return pl.BlockSpec((TM, width), lambda i: (_prompt_block(i), 0))


def _sample_spec(width):
    return pl.BlockSpec((TM, width), lambda i: (_sample_block(i), 0))


def _prompt_state_spec(rows, width):
    return pl.BlockSpec((rows, width), lambda i: (_prompt_block(i) // BLOCKS_PER_SEQ, 0))


_PARAMS = pltpu.CompilerParams(dimension_semantics=("arbitrary",), vmem_limit_bytes=VMEM_LIMIT)


def _is_sample():
    return pl.program_id(0) >= N_PROMPT_BLOCKS


def _read_tokens(split, refs):
    if not split:
        return refs[0][...]
    return jnp.where(_is_sample(), refs[1][...], refs[0][...])


def _write_by_side(prompt_ref, sample_ref, prompt_val, sample_val):
    @pl.when(jnp.logical_not(_is_sample()))
    def _():
        prompt_ref[...] = prompt_val

    @pl.when(_is_sample())
    def _():
        sample_ref[...] = sample_val


def _mixer_a_kernel(split_in, *refs):
    n_x = 2 if split_in else 1
    x_refs = refs[:n_x]
    (w_in_ref, b_in_ref, lng_ref, lnb_ref, ws_ref, bs_ref, w_out_ref, g1_ref, b1_ref,
     x1_ref, vp_ref, vs_ref, t_ref) = refs[n_x:]
    x = _read_tokens(split_in, x_refs)
    xb = x.astype(_BF16)
    u = _gelu(_dot(xb, w_in_ref[:, :A_WIDTH]) + b_in_ref[:, :A_WIDTH])
    v = _gelu(_dot(xb, w_in_ref[:, A_WIDTH:]) + b_in_ref[:, A_WIDTH:])
    v = _layer_norm(v, lng_ref[...], lnb_ref[...])
    _write_by_side(vp_ref, vs_ref, v[TM - CHUNK:, :], v)
    vb = v.astype(_BF16)

    r = lax.broadcasted_iota(_I32, (CHUNK, CHUNK), 0)
    c = lax.broadcasted_iota(_I32, (CHUNK, CHUNK), 1)
    shift = jnp.where(_is_sample(), int(math.log2(DEC_SEQ)), int(math.log2(CHUNK)))
    mask = jnp.logical_and(r >= c, (r >> shift) == (c >> shift))
    for h in range(A_HEADS):
        w = jnp.where(mask, ws_ref[0, h], 0.0).astype(_BF16)
        bias = bs_ref[0, :, h:h + 1]
        cols = slice(h * A_HEAD_DIM, (h + 1) * A_HEAD_DIM)
        for k in range(TM // CHUNK):
            rows = slice(k * CHUNK, (k + 1) * CHUNK)
            sg = _dot(w, vb[rows, cols]) + bias
            t_ref[rows, cols] = (u[rows, cols] * sg).astype(_BF16)
    y = _dot(t_ref[...], w_out_ref[...])
    x1_ref[...] = _layer_norm(DEEPNORM_ALPHA * x + y, g1_ref[...], b1_ref[...])


def _mixer_a(xs, w_in, b_in, ln_g, ln_b, ws2, bs2, w_out, g1, b1):
    split_in = len(xs) == 2
    x_specs = [_prompt_spec(D_MODEL), _sample_spec(D_MODEL)] if split_in else [_tok_spec(D_MODEL)]
    side = lambda i: jnp.where(i < N_PROMPT_BLOCKS, 0, 1)
    return pl.pallas_call(
        functools.partial(_mixer_a_kernel, split_in),
        grid=(N_BLOCKS,),
        in_specs=x_specs + [
            _const_spec((D_MODEL, 2 * A_WIDTH)),
            _const_spec((1, 2 * A_WIDTH)),
            _const_spec((1, A_WIDTH)),
            _const_spec((1, A_WIDTH)),
            pl.BlockSpec((1, A_HEADS, CHUNK, CHUNK), lambda i: (side(i), 0, 0, 0)),
            pl.BlockSpec((1, CHUNK, V7X_LANES), lambda i: (side(i), 0, 0)),
            _const_spec((A_WIDTH, D_MODEL)),
            _const_spec((1, D_MODEL)),
            _const_spec((1, D_MODEL)),
        ],
        out_specs=[
            _tok_spec(D_MODEL),
            _prompt_state_spec(CHUNK, A_WIDTH),
            _sample_spec(A_WIDTH),
        ],
        out_shape=[
            jax.ShapeDtypeStruct((N_TOK, D_MODEL), _F32),
            jax.ShapeDtypeStruct((BATCH * CHUNK, A_WIDTH), _F32),
            jax.ShapeDtypeStruct((N_SAMPLE_TOK, A_WIDTH), _F32),
        ],
        scratch_shapes=[pltpu.VMEM((TM, A_WIDTH), _BF16)],
        compiler_params=_PARAMS,
        name="mixer_a",
    )(*xs, w_in, b_in, ln_g, ln_b, ws2, bs2, w_out, g1, b1)


def _mixer_b_kernel(x_ref, w_in_ref, cw_ref, w_out_ref, g1_ref, b1_ref, past1_ref, past2_ref,
                    x1_ref, stp_ref, sts_ref, ext_ref):
    i = pl.program_id(0)
    is_sample = _is_sample()
    x = x_ref[...]
    h = _dot(x.astype(_BF16), w_in_ref[...])
    gate_b = h[:, :B_WIDTH]
    p = h[:, B_WIDTH:2 * B_WIDTH] * h[:, 2 * B_WIDTH:]
    _write_by_side(stp_ref, sts_ref, p[TM - V7X_SUBLANES:, :], p)

    @pl.when(jnp.logical_or(is_sample, i % BLOCKS_PER_SEQ == 0))
    def _():
        ext_ref[0:V7X_SUBLANES, :] = jnp.zeros((V7X_SUBLANES, B_WIDTH), _F32)

    ext_ref[V7X_SUBLANES:, :] = p
    prev1 = ext_ref[V7X_SUBLANES - 1:V7X_SUBLANES - 1 + TM, :]
    prev2 = ext_ref[V7X_SUBLANES - 2:V7X_SUBLANES - 2 + TM, :]
    pos = lax.broadcasted_iota(_I32, (TM, B_WIDTH), 0) % DEC_SEQ
    pos = jnp.where(is_sample, pos, DEC_SEQ)
    prev1 = jnp.where(pos < 1, past1_ref[...], prev1)
    prev2 = jnp.where(pos < 2, past2_ref[...], prev2)
    conv = cw_ref[0:1, :] * prev2 + cw_ref[1:2, :] * prev1 + cw_ref[2:3, :] * p
    y = _dot((gate_b * conv).astype(_BF16), w_out_ref[...])
    x1_ref[...] = _layer_norm(DEEPNORM_ALPHA * x + y, g1_ref[...], b1_ref[...])
    ext_ref[0:V7X_SUBLANES, :] = p[TM - V7X_SUBLANES:, :]


def _mixer_b(x, w_in, conv_w, w_out, g1, b1, past1, past2):
    return pl.pallas_call(
        _mixer_b_kernel,
        grid=(N_BLOCKS,),
        in_specs=[
            _tok_spec(D_MODEL),
            _const_spec((D_MODEL, 3 * B_WIDTH)),
            _const_spec((V7X_SUBLANES, B_WIDTH)),
            _const_spec((B_WIDTH, D_MODEL)),
            _const_spec((1, D_MODEL)),
            _const_spec((1, D_MODEL)),
            _sample_spec(B_WIDTH),
            _sample_spec(B_WIDTH),
        ],
        out_specs=[
            _tok_spec(D_MODEL),
            _prompt_state_spec(V7X_SUBLANES, B_WIDTH),
            _sample_spec(B_WIDTH),
        ],
        out_shape=[
            jax.ShapeDtypeStruct((N_TOK, D_MODEL), _F32),
            jax.ShapeDtypeStruct((BATCH * V7X_SUBLANES, B_WIDTH), _F32),
            jax.ShapeDtypeStruct((N_SAMPLE_TOK, B_WIDTH), _F32),
        ],
        scratch_shapes=[pltpu.VMEM((V7X_SUBLANES + TM, B_WIDTH), _F32)],
        compiler_params=_PARAMS,
        name="mixer_b",
    )(x, w_in, conv_w, w_out, g1, b1, past1, past2)


def _route(logits):
    lane = lax.broadcasted_iota(_I32, logits.shape, 1)
    lane_f = lane.astype(_F32)
    neg = jnp.float32(-jnp.inf)
    far = jnp.float32(ROUTER_LANES)

    def first_argmax(vals):
        top = jnp.max(vals, axis=-1, keepdims=True)
        idx = jnp.min(jnp.where(vals == top, lane_f, far), axis=-1, keepdims=True)
        return top, idx

    is_group = lane < N_GROUPS
    g_top, g_idx = first_argmax(jnp.where(is_group, logits, neg))
    g_w = 1.0 / jnp.sum(jnp.where(is_group, jnp.exp(logits - g_top), 0.0), axis=-1, keepdims=True)
    lo = N_GROUPS + EXPERTS_PER_GROUP * g_idx
    in_group = jnp.logical_and(lane_f >= lo, lane_f < lo + EXPERTS_PER_GROUP)
    e_logits = jnp.where(in_group, logits, neg)
    v1, i1 = first_argmax(e_logits)
    v2, i2 = first_argmax(jnp.where(lane_f == i1, neg, e_logits))
    e2 = jnp.exp(v2 - v1)
    w1 = g_w / (1.0 + e2)
    w2 = g_w * e2 / (1.0 + e2)
    return jnp.where(lane_f == i1, w1, 0.0) + jnp.where(lane_f == i2, w2, 0.0)


def _moe_ple_kernel(split_out, x1_ref, wr_ref, br_ref, wgu_ref, wd_ref, g2_ref, b2_ref, wg_ref, bg_ref,
                    wp_ref, pp_ref, ps_ref, *rest):
    out_refs, a_ref = rest[:-1], rest[-1]
    x1 = x1_ref[...]
    xb = x1.astype(_BF16)
    combine = _route(_dot(xb, wr_ref[...]) + br_ref[...])
    for e in range(N_EXPERTS):
        hgu = _dot(xb, wgu_ref[e])
        gate = hgu[:, :D_EXPERT]
        up = hgu[:, D_EXPERT:]
        w = combine[:, N_GROUPS + e:N_GROUPS + e + 1]
        a_ref[:, e * D_EXPERT:(e + 1) * D_EXPERT] = (gate * jax.nn.sigmoid(gate) * up * w).astype(_BF16)
    moe = _dot(a_ref[...], wd_ref[...])
    x2 = _layer_norm(DEEPNORM_ALPHA * x1 + moe, g2_ref[...], b2_ref[...])
    gate = jax.nn.sigmoid(_dot(x2.astype(_BF16), wg_ref[...]) + bg_ref[...])
    p = jnp.where(_is_sample(), ps_ref[0], pp_ref[0])
    out = x2 + gate * _dot(p.astype(_BF16), wp_ref[...])
    if split_out:
        _write_by_side(out_refs[0], out_refs[1], out, out)
    else:
        out_refs[0][...] = out


def _moe_ple(layer, split_out, x1, wr, br, wgu, wd, g2, b2, wg, bg, wp, p_prompt, p_sample):
    if split_out:
        out_specs = [_prompt_spec(D_MODEL), _sample_spec(D_MODEL)]
        out_shape = [jax.ShapeDtypeStruct((N_PROMPT_TOK, D_MODEL), _F32),
                     jax.ShapeDtypeStruct((N_SAMPLE_TOK, D_MODEL), _F32)]
    else:
        out_specs = [_tok_spec(D_MODEL)]
        out_shape = [jax.ShapeDtypeStruct((N_TOK, D_MODEL), _F32)]
    return pl.pallas_call(
        functools.partial(_moe_ple_kernel, split_out),
        grid=(N_BLOCKS,),
        in_specs=[
            _tok_spec(D_MODEL),
            _const_spec((D_MODEL, ROUTER_LANES)),
            _const_spec((1, ROUTER_LANES)),
            _const_spec((N_EXPERTS, D_MODEL, 2 * D_EXPERT)),
            _const_spec((N_EXPERTS * D_EXPERT, D_MODEL)),
            _const_spec((1, D_MODEL)),
            _const_spec((1, D_MODEL)),
            _const_spec((D_MODEL, D_MODEL)),
            _const_spec((1, D_MODEL)),
            _const_spec((PLE_DIM, D_MODEL)),
            pl.BlockSpec((1, TM, PLE_DIM), lambda i: (layer, _prompt_block(i), 0)),
            pl.BlockSpec((1, TM, PLE_DIM), lambda i: (layer, _sample_block(i), 0)),
        ],
        out_specs=out_specs,
        out_shape=out_shape,
        scratch_shapes=[pltpu.VMEM((TM, N_EXPERTS * D_EXPERT), _BF16)],
        compiler_params=_PARAMS,
        name="moe_ple",
    )(x1, wr, br, wgu, wd, g2, b2, wg, bg, wp, p_prompt, p_sample)


def _row(v):
    return v.reshape(1, -1)


def _pad_lanes(a, lanes):
    return jnp.pad(a, [(0, 0)] * (a.ndim - 1) + [(0, lanes - a.shape[-1])])


def _pad_rows(a):
    return jnp.pad(a, [(0, V7X_SUBLANES - a.shape[0]), (0, 0)])


def kernel(x_prompt, x_sample, state_conv, p_prompt, p_sample, a_w_in, a_b_in, a_ln_g, a_ln_b, a_w_s, a_b_s, a_w_out, b_w_in, b_conv_w, b_w_out, ln1_g, ln1_b, ln2_g, ln2_b, moe_w_group, moe_b_group, moe_w_expert, moe_b_expert, moe_w_gate_up, moe_w_down, ple_w_gate, ple_b_gate, ple_w_proj):
    xs = (x_prompt.reshape(N_PROMPT_TOK, D_MODEL), x_sample.reshape(N_SAMPLE_TOK, D_MODEL))
    pp = p_prompt.reshape(DEPTH, N_PROMPT_TOK, PLE_DIM)
    ps = p_sample.reshape(DEPTH, N_SAMPLE_TOK, PLE_DIM)

    reps = CHUNK // DEC_SEQ
    ws_sample = jnp.tile(a_w_s[:, :, :DEC_SEQ, :DEC_SEQ], (1, 1, reps, reps))
    ws2 = jnp.stack([a_w_s, ws_sample], axis=1)
    bs_sample = jnp.tile(a_b_s[:, :, :DEC_SEQ], (1, 1, reps))
    bs2 = _pad_lanes(jnp.swapaxes(jnp.stack([a_b_s, bs_sample], axis=1), -1, -2), V7X_LANES)

    zeros = jnp.zeros((state_conv.shape[0], DEC_BATCH, DEC_SEQ, B_WIDTH), _F32)
    past1 = zeros.at[:, :, 0].set(state_conv[:, :, 1]).reshape(-1, N_SAMPLE_TOK, B_WIDTH)
    past2 = (zeros.at[:, :, 0].set(state_conv[:, :, 0]).at[:, :, 1].set(state_conv[:, :, 1])
             .reshape(-1, N_SAMPLE_TOK, B_WIDTH))

    w_router = _pad_lanes(jnp.concatenate([moe_w_group, moe_w_expert], axis=-1), ROUTER_LANES).astype(_BF16)
    b_router = _pad_lanes(jnp.concatenate([moe_b_group, moe_b_expert], axis=-1), ROUTER_LANES)

    v_prompt, v_sample, st_prompt, st_sample = [], [], [], []
    for i in range(DEPTH):
        j = i // 2
        if i % 2 == 0:
            x1, vp, vs = _mixer_a(xs, a_w_in[j].astype(_BF16), _row(a_b_in[j]), _row(a_ln_g[j]),
                                  _row(a_ln_b[j]), ws2[j], bs2[j], a_w_out[j].astype(_BF16),
                                  _row(ln1_g[i]), _row(ln1_b[i]))
            v_prompt.append(vp)
            v_sample.append(vs)
        else:
            x1, stp, sts = _mixer_b(xs[0], b_w_in[j].astype(_BF16), _pad_rows(b_conv_w[j]),
                                    b_w_out[j].astype(_BF16), _row(ln1_g[i]), _row(ln1_b[i]), past1[j], past2[j])
            st_prompt.append(stp)
            st_sample.append(sts)
        xs = _moe_ple(i, i == DEPTH - 1, x1, w_router[i], _row(b_router[i]), moe_w_gate_up[i].astype(_BF16),
                      moe_w_down[i].astype(_BF16).reshape(N_EXPERTS * D_EXPERT, D_MODEL),
                      _row(ln2_g[i]), _row(ln2_b[i]), ple_w_gate[i].astype(_BF16), _row(ple_b_gate[i]),
                      ple_w_proj[i].astype(_BF16), pp, ps)

    keep = CONV_WIDTH - 1
    return (xs[0].reshape(BATCH, SEQ, D_MODEL),
            xs[1].reshape(DEC_BATCH, DEC_SEQ, D_MODEL),
            jnp.stack(v_prompt).reshape(-1, BATCH, CHUNK, A_WIDTH),
            jnp.stack(v_sample).reshape(-1, DEC_BATCH, DEC_SEQ, A_WIDTH),
            jnp.stack(st_prompt).reshape(-1, BATCH, V7X_SUBLANES, B_WIDTH)[:, :, V7X_SUBLANES - keep:],
            jnp.stack(st_sample).reshape(-1, DEC_BATCH, DEC_SEQ, B_WIDTH)[:, :, DEC_SEQ - keep:])
```

```python
import functools
import math

import jax
import jax.numpy as jnp
from jax import lax
from jax.experimental import pallas as pl
from jax.experimental.pallas import tpu as pltpu

D_MODEL = 1024
BATCH = 8
SEQ = 2048
DEPTH = 4
DEC_BATCH = 128
DEC_SEQ = 8
CHUNK = 128
A_HEADS = 4
A_WIDTH = 3 * D_MODEL
A_HEAD_DIM = A_WIDTH // A_HEADS
CONV_WIDTH = 3
B_WIDTH = D_MODEL
N_GROUPS = 4
EXPERTS_PER_GROUP = 4
N_EXPERTS = N_GROUPS * EXPERTS_PER_GROUP
D_EXPERT = D_MODEL // 4
PLE_DIM = 256
LN_EPS = 1e-5
DEEPNORM_ALPHA = (2 * DEPTH) ** 0.25

V7X_LANES = 128
V7X_SUBLANES = 8
V7X_VMEM_BYTES = 64 * 1024 * 1024

N_PROMPT_TOK = BATCH * SEQ
N_SAMPLE_TOK = DEC_BATCH * DEC_SEQ
N_TOK = N_PROMPT_TOK + N_SAMPLE_TOK
TM = 256
N_PROMPT_BLOCKS = N_PROMPT_TOK // TM
N_SAMPLE_BLOCKS = N_SAMPLE_TOK // TM
N_BLOCKS = N_PROMPT_BLOCKS + N_SAMPLE_BLOCKS
BLOCKS_PER_SEQ = SEQ // TM
ROUTER_LANES = V7X_LANES
VMEM_LIMIT = V7X_VMEM_BYTES - 4 * 1024 * 1024

MOE_TM = 512
SUB = 256
N_SUB = MOE_TM // SUB
SLOTS = 64
SUB_SORTED = N_EXPERTS * SLOTS
N_PROMPT_MOE_BLOCKS = N_PROMPT_TOK // MOE_TM

assert N_PROMPT_TOK % MOE_TM == 0 and N_SAMPLE_TOK % MOE_TM == 0 and MOE_TM % SUB == 0
assert SEQ % TM == 0 and N_SAMPLE_TOK % TM == 0 and TM % CHUNK == 0 and TM % DEC_SEQ == 0
assert CHUNK % DEC_SEQ == 0 and CONV_WIDTH - 1 <= V7X_SUBLANES

_BF16 = jnp.bfloat16
_F32 = jnp.float32
_I32 = jnp.int32


def _layer_norm(x, g, b):
    mu = jnp.mean(x, axis=-1, keepdims=True)
    xc = x - mu
    var = jnp.mean(xc * xc, axis=-1, keepdims=True)
    return xc * lax.rsqrt(var + LN_EPS) * g + b


def _gelu(x):
    return 0.5 * x * (1.0 + lax.erf(x * math.sqrt(0.5)))


def _dot(a, b):
    return jnp.dot(a, b, preferred_element_type=_F32)


def _const_spec(shape):
    zeros = (0,) * len(shape)
    return pl.BlockSpec(shape, lambda i: zeros, pipeline_mode=pl.Buffered(1))


def _tok_spec(width):
    return pl.BlockSpec((TM, width), lambda i: (i, 0))


def _prompt_block(i):
    return jnp.minimum(i, N_PROMPT_BLOCKS - 1)


def _sample_block(i):
    return jnp.maximum(i - N_PROMPT_BLOCKS, 0)


def _prompt_spec(width):
    return pl.BlockSpec((TM, width), lambda i: (_prompt_block(i), 0))


def _sample_spec(width):
    return pl.BlockSpec((TM, width), lambda i: (_sample_block(i), 0))


def _prompt_state_spec(rows, width):
    return pl.BlockSpec((rows, width), lambda i: (_prompt_block(i) // BLOCKS_PER_SEQ, 0))


_PARAMS = pltpu.CompilerParams(dimension_semantics=("arbitrary",), vmem_limit_bytes=VMEM_LIMIT)


def _is_sample():
    return pl.program_id(0) >= N_PROMPT_BLOCKS


def _read_tokens(split, refs):
    if not split:
        return refs[0][...]
    return jnp.where(_is_sample(), refs[1][...], refs[0][...])


def _write_by_side(prompt_ref, sample_ref, prompt_val, sample_val):
    @pl.when(jnp.logical_not(_is_sample()))
    def _():
        prompt_ref[...] = prompt_val

    @pl.when(_is_sample())
    def _():
        sample_ref[...] = sample_val


def _mixer_a_kernel(split_in, *refs):
    n_x = 2 if split_in else 1
    x_refs = refs[:n_x]
    (w_in_ref, b_in_ref, lng_ref, lnb_ref, ws_ref, bs_ref, w_out_ref, g1_ref, b1_ref,
     x1_ref, vp_ref, vs_ref, t_ref) = refs[n_x:]
    x = _read_tokens(split_in, x_refs)
    xb = x.astype(_BF16)
    u = _gelu(_dot(xb, w_in_ref[:, :A_WIDTH]) + b_in_ref[:, :A_WIDTH])
    v = _gelu(_dot(xb, w_in_ref[:, A_WIDTH:]) + b_in_ref[:, A_WIDTH:])
    v = _layer_norm(v, lng_ref[...], lnb_ref[...])
    _write_by_side(vp_ref, vs_ref, v[TM - CHUNK:, :], v)
    vb = v.astype(_BF16)

    r = lax.broadcasted_iota(_I32, (CHUNK, CHUNK), 0)
    c = lax.broadcasted_iota(_I32, (CHUNK, CHUNK), 1)
    shift = jnp.where(_is_sample(), int(math.log2(DEC_SEQ)), int(math.log2(CHUNK)))
    mask = jnp.logical_and(r >= c, (r >> shift) == (c >> shift))
    for h in range(A_HEADS):
        w = jnp.where(mask, ws_ref[0, h], 0.0).astype(_BF16)
        bias = bs_ref[0, :, h:h + 1]
        cols = slice(h * A_HEAD_DIM, (h + 1) * A_HEAD_DIM)
        for k in range(TM // CHUNK):
            rows = slice(k * CHUNK, (k + 1) * CHUNK)
            sg = _dot(w, vb[rows, cols]) + bias
            t_ref[rows, cols] = (u[rows, cols] * sg).astype(_BF16)
    y = _dot(t_ref[...], w_out_ref[...])
    x1_ref[...] = _layer_norm(DEEPNORM_ALPHA * x + y, g1_ref[...], b1_ref[...])


def _mixer_a(xs, w_in, b_in, ln_g, ln_b, ws2, bs2, w_out, g1, b1):
    split_in = len(xs) == 2
    x_specs = [_prompt_spec(D_MODEL), _sample_spec(D_MODEL)] if split_in else [_tok_spec(D_MODEL)]
    side = lambda i: jnp.where(i < N_PROMPT_BLOCKS, 0, 1)
    return pl.pallas_call(
        functools.partial(_mixer_a_kernel, split_in),
        grid=(N_BLOCKS,),
        in_specs=x_specs + [
            _const_spec((D_MODEL, 2 * A_WIDTH)),
            _const_spec((1, 2 * A_WIDTH)),
            _const_spec((1, A_WIDTH)),
            _const_spec((1, A_WIDTH)),
            pl.BlockSpec((1, A_HEADS, CHUNK, CHUNK), lambda i: (side(i), 0, 0, 0)),
            pl.BlockSpec((1, CHUNK, V7X_LANES), lambda i: (side(i), 0, 0)),
            _const_spec((A_WIDTH, D_MODEL)),
            _const_spec((1, D_MODEL)),
            _const_spec((1, D_MODEL)),
        ],
        out_specs=[
            _tok_spec(D_MODEL),
            _prompt_state_spec(CHUNK, A_WIDTH),
            _sample_spec(A_WIDTH),
        ],
        out_shape=[
            jax.ShapeDtypeStruct((N_TOK, D_MODEL), _F32),
            jax.ShapeDtypeStruct((BATCH * CHUNK, A_WIDTH), _F32),
            jax.ShapeDtypeStruct((N_SAMPLE_TOK, A_WIDTH), _F32),
        ],
        scratch_shapes=[pltpu.VMEM((TM, A_WIDTH), _BF16)],
        compiler_params=_PARAMS,
        name="mixer_a",
    )(*xs, w_in, b_in, ln_g, ln_b, ws2, bs2, w_out, g1, b1)


def _mixer_b_kernel(x_ref, w_in_ref, cw_ref, w_out_ref, g1_ref, b1_ref, past1_ref, past2_ref,
                    x1_ref, stp_ref, sts_ref, ext_ref):
    i = pl.program_id(0)
    is_sample = _is_sample()
    x = x_ref[...]
    h = _dot(x.astype(_BF16), w_in_ref[...])
    gate_b = h[:, :B_WIDTH]
    p = h[:, B_WIDTH:2 * B_WIDTH] * h[:, 2 * B_WIDTH:]
    _write_by_side(stp_ref, sts_ref, p[TM - V7X_SUBLANES:, :], p)

    @pl.when(jnp.logical_or(is_sample, i % BLOCKS_PER_SEQ == 0))
    def _():
        ext_ref[0:V7X_SUBLANES, :] = jnp.zeros((V7X_SUBLANES, B_WIDTH), _F32)

    ext_ref[V7X_SUBLANES:, :] = p
    prev1 = ext_ref[V7X_SUBLANES - 1:V7X_SUBLANES - 1 + TM, :]
    prev2 = ext_ref[V7X_SUBLANES - 2:V7X_SUBLANES - 2 + TM, :]
    pos = lax.broadcasted_iota(_I32, (TM, B_WIDTH), 0) % DEC_SEQ
    pos = jnp.where(is_sample, pos, DEC_SEQ)
    prev1 = jnp.where(pos < 1, past1_ref[...], prev1)
    prev2 = jnp.where(pos < 2, past2_ref[...], prev2)
    conv = cw_ref[0:1, :] * prev2 + cw_ref[1:2, :] * prev1 + cw_ref[2:3, :] * p
    y = _dot((gate_b * conv).astype(_BF16), w_out_ref[...])
    x1_ref[...] = _layer_norm(DEEPNORM_ALPHA * x + y, g1_ref[...], b1_ref[...])
    ext_ref[0:V7X_SUBLANES, :] = p[TM - V7X_SUBLANES:, :]


def _mixer_b(x, w_in, conv_w, w_out, g1, b1, past1, past2):
    return pl.pallas_call(
        _mixer_b_kernel,
        grid=(N_BLOCKS,),
        in_specs=[
            _tok_spec(D_MODEL),
            _const_spec((D_MODEL, 3 * B_WIDTH)),
            _const_spec((V7X_SUBLANES, B_WIDTH)),
            _const_spec((B_WIDTH, D_MODEL)),
            _const_spec((1, D_MODEL)),
            _const_spec((1, D_MODEL)),
            _sample_spec(B_WIDTH),
            _sample_spec(B_WIDTH),
        ],
        out_specs=[
            _tok_spec(D_MODEL),
            _prompt_state_spec(V7X_SUBLANES, B_WIDTH),
            _sample_spec(B_WIDTH),
        ],
        out_shape=[
            jax.ShapeDtypeStruct((N_TOK, D_MODEL), _F32),
            jax.ShapeDtypeStruct((BATCH * V7X_SUBLANES, B_WIDTH), _F32),
            jax.ShapeDtypeStruct((N_SAMPLE_TOK, B_WIDTH), _F32),
        ],
        scratch_shapes=[pltpu.VMEM((V7X_SUBLANES + TM, B_WIDTH), _F32)],
        compiler_params=_PARAMS,
        name="mixer_b",
    )(x, w_in, conv_w, w_out, g1, b1, past1, past2)


def _route(logits):
    lane = lax.broadcasted_iota(_I32, logits.shape, 1)
    lane_f = lane.astype(_F32)
    neg = jnp.float32(-jnp.inf)
    far = jnp.float32(ROUTER_LANES)

    def first_argmax(vals):
        top = jnp.max(vals, axis=-1, keepdims=True)
        idx = jnp.min(jnp.where(vals == top, lane_f, far), axis=-1, keepdims=True)
        return top, idx

    is_group = lane < N_GROUPS
    g_top, g_idx = first_argmax(jnp.where(is_group, logits, neg))
    g_w = 1.0 / jnp.sum(jnp.where(is_group, jnp.exp(logits - g_top), 0.0), axis=-1, keepdims=True)
    lo = N_GROUPS + EXPERTS_PER_GROUP * g_idx
    in_group = jnp.logical_and(lane_f >= lo, lane_f < lo + EXPERTS_PER_GROUP)
    e_logits = jnp.where(in_group, logits, neg)
    v1, i1 = first_argmax(e_logits)
    v2, i2 = first_argmax(jnp.where(lane_f == i1, neg, e_logits))
    e2 = jnp.exp(v2 - v1)
    return i1, i2, g_w / (1.0 + e2), g_w * e2 / (1.0 + e2)


def _expert_act(hgu, scale):
    gate = hgu[:, :D_EXPERT]
    return (gate * jax.nn.sigmoid(gate) * hgu[:, D_EXPERT:] * scale).astype(_BF16)


def _moe_ple_kernel(split_out, x1_ref, wr_ref, br_ref, wgu_ref, wd_ref, g2_ref, b2_ref, wg_ref, bg_ref,
                    wp_ref, pp_ref, ps_ref, *rest):
    n_out = 2 if split_out else 1
    out_refs = rest[:n_out]
    xs_ref, ws_ref, sc_ref, ylo_ref, flag_v, flag_s, sem = rest[n_out:]
    yhi_ref = xs_ref
    is_sample = pl.program_id(0) >= N_PROMPT_MOE_BLOCKS
    x1 = x1_ref[...]
    xb = x1.astype(_BF16)
    i1, i2, w1, w2 = _route(_dot(xb, wr_ref[...]) + br_ref[...])

    lane = lax.broadcasted_iota(_I32, (SUB, ROUTER_LANES), 1)
    lane_f = lane.astype(_F32)
    r = lax.broadcasted_iota(_I32, (SUB, SUB), 0)
    c = lax.broadcasted_iota(_I32, (SUB, SUB), 1)
    earlier = jnp.where(r > c, 1.0, 0.0).astype(_BF16)
    sorted_row = lax.broadcasted_iota(_I32, (SUB_SORTED, SUB), 0).astype(_F32)
    sorted_lane = lax.broadcasted_iota(_I32, (SUB, SUB_SORTED), 1).astype(_F32)
    fullest = jnp.zeros((1, ROUTER_LANES), _F32)
    for s in range(N_SUB):
        rows = slice(s * SUB, (s + 1) * SUB)
        hit1 = lane_f == i1[rows]
        hit2 = lane_f == i2[rows]
        onehot = jnp.where(jnp.logical_or(hit1, hit2), 1.0, 0.0)
        fullest = jnp.maximum(fullest, jnp.sum(onehot, axis=0, keepdims=True))
        rank = _dot(earlier, onehot.astype(_BF16))
        q1 = (i1[rows] - N_GROUPS) * SLOTS + jnp.sum(jnp.where(hit1, rank, 0.0), axis=-1, keepdims=True)
        q2 = (i2[rows] - N_GROUPS) * SLOTS + jnp.sum(jnp.where(hit2, rank, 0.0), axis=-1, keepdims=True)
        as_rows = jnp.where(lane == 0, q1, jnp.where(lane == 1, q2, jnp.where(lane == 2, w1[rows],
                            jnp.where(lane == 3, w2[rows], 0.0)))).T
        at1 = sorted_row == as_rows[0:1, :]
        at2 = sorted_row == as_rows[1:2, :]
        gather = jnp.where(jnp.logical_or(at1, at2), 1.0, 0.0).astype(_BF16)
        xs_ref[s] = _dot(gather, xb[rows]).astype(_BF16)
        weight = jnp.sum(jnp.where(at1, as_rows[2:3, :], 0.0) + jnp.where(at2, as_rows[3:4, :], 0.0),
                         axis=-1, keepdims=True)
        ws_ref[s] = jnp.broadcast_to(weight, (SUB_SORTED, V7X_LANES))
        sc_ref[s] = jnp.where(jnp.logical_or(sorted_lane == q1, sorted_lane == q2), 1.0, 0.0).astype(_BF16)

    flag_v[...] = jnp.broadcast_to(jnp.max(fullest, axis=-1, keepdims=True), flag_v.shape).astype(_I32)
    to_smem = pltpu.make_async_copy(flag_v, flag_s, sem)
    to_smem.start()
    to_smem.wait()
    fits = flag_s[0, 0] <= SLOTS

    def finish(rows, moe):
        x2 = _layer_norm(DEEPNORM_ALPHA * x1_ref[rows, :] + moe, g2_ref[...], b2_ref[...])
        gate = jax.nn.sigmoid(_dot(x2.astype(_BF16), wg_ref[...]) + bg_ref[...])
        p = jnp.where(is_sample, ps_ref[0, rows, :], pp_ref[0, rows, :])
        out = x2 + gate * _dot(p.astype(_BF16), wp_ref[...])
        if split_out:
            @pl.when(jnp.logical_not(is_sample))
            def _():
                out_refs[0][rows, :] = out

            @pl.when(is_sample)
            def _():
                out_refs[1][rows, :] = out
        else:
            out_refs[0][rows, :] = out

    @pl.when(fits)
    def _():
        for e in range(N_EXPERTS):
            slots = slice(e * SLOTS, (e + 1) * SLOTS)
            xt = jnp.concatenate([xs_ref[s, slots, :] for s in range(N_SUB)], axis=0)
            wt = jnp.concatenate([ws_ref[s, slots, 0:1] for s in range(N_SUB)], axis=0)
            y = _dot(_expert_act(_dot(xt, wgu_ref[e]), wt), wd_ref[e])
            hi = y.astype(_BF16)
            lo = (y - hi.astype(_F32)).astype(_BF16)
            for s in range(N_SUB):
                yhi_ref[s, slots, :] = hi[s * SLOTS:(s + 1) * SLOTS, :]
                ylo_ref[s, slots, :] = lo[s * SLOTS:(s + 1) * SLOTS, :]
        for s in range(N_SUB):
            finish(slice(s * SUB, (s + 1) * SUB), _dot(sc_ref[s], yhi_ref[s]) + _dot(sc_ref[s], ylo_ref[s]))

    @pl.when(jnp.logical_not(fits))
    def _():
        lane_all = lax.broadcasted_iota(_I32, (MOE_TM, ROUTER_LANES), 1).astype(_F32)
        combine = jnp.where(lane_all == i1, w1, 0.0) + jnp.where(lane_all == i2, w2, 0.0)
        acc = jnp.zeros((MOE_TM, D_MODEL), _F32)
        for e in range(N_EXPERTS):
            w = combine[:, N_GROUPS + e:N_GROUPS + e + 1]
            acc = acc + _dot(_expert_act(_dot(xb, wgu_ref[e]), w), wd_ref[e])
        finish(slice(0, MOE_TM), acc)


def _moe_ple(layer, split_out, x1, wr, br, wgu, wd, g2, b2, wg, bg, wp, p_prompt, p_sample):
    prompt_block = lambda i: jnp.minimum(i, N_PROMPT_MOE_BLOCKS - 1)
    sample_block = lambda i: jnp.maximum(i - N_PROMPT_MOE_BLOCKS, 0)
    if split_out:
        out_specs = [pl.BlockSpec((MOE_TM, D_MODEL), lambda i: (prompt_block(i), 0)),
                     pl.BlockSpec((MOE_TM, D_MODEL), lambda i: (sample_block(i), 0))]
        out_shape = [jax.ShapeDtypeStruct((N_PROMPT_TOK, D_MODEL), _F32),
                     jax.ShapeDtypeStruct((N_SAMPLE_TOK, D_MODEL), _F32)]
    else:
        out_specs = [pl.BlockSpec((MOE_TM, D_MODEL), lambda i: (i, 0))]
        out_shape = [jax.ShapeDtypeStruct((N_TOK, D_MODEL), _F32)]
    return pl.pallas_call(
        functools.partial(_moe_ple_kernel, split_out),
        grid=(N_TOK // MOE_TM,),
        in_specs=[
            pl.BlockSpec((MOE_TM, D_MODEL), lambda i: (i, 0)),
            _const_spec((D_MODEL, ROUTER_LANES)),
            _const_spec((1, ROUTER_LANES)),
            _const_spec((N_EXPERTS, D_MODEL, 2 * D_EXPERT)),
            _const_spec((N_EXPERTS, D_EXPERT, D_MODEL)),
            _const_spec((1, D_MODEL)),
            _const_spec((1, D_MODEL)),
            _const_spec((D_MODEL, D_MODEL)),
            _const_spec((1, D_MODEL)),
            _const_spec((PLE_DIM, D_MODEL)),
            pl.BlockSpec((1, MOE_TM, PLE_DIM), lambda i: (layer, prompt_block(i), 0)),
            pl.BlockSpec((1, MOE_TM, PLE_DIM), lambda i: (layer, sample_block(i), 0)),
        ],
        out_specs=out_specs,
        out_shape=out_shape,
        scratch_shapes=[
            pltpu.VMEM((N_SUB, SUB_SORTED, D_MODEL), _BF16),
            pltpu.VMEM((N_SUB, SUB_SORTED, V7X_LANES), _F32),
            pltpu.VMEM((N_SUB, SUB, SUB_SORTED), _BF16),
            pltpu.VMEM((N_SUB, SUB_SORTED, D_MODEL), _BF16),
            pltpu.VMEM((V7X_SUBLANES, V7X_LANES), _I32),
            pltpu.SMEM((V7X_SUBLANES, V7X_LANES), _I32),
            pltpu.SemaphoreType.DMA,
        ],
        compiler_params=_PARAMS,
        name="moe_ple",
    )(x1, wr, br, wgu, wd, g2, b2, wg, bg, wp, p_prompt, p_sample)


def _row(v):
    return v.reshape(1, -1)


def _pad_lanes(a, lanes):
    return jnp.pad(a, [(0, 0)] * (a.ndim - 1) + [(0, lanes - a.shape[-1])])


def _pad_rows(a):
    return jnp.pad(a, [(0, V7X_SUBLANES - a.shape[0]), (0, 0)])


def kernel(x_prompt, x_sample, state_conv, p_prompt, p_sample, a_w_in, a_b_in, a_ln_g, a_ln_b, a_w_s, a_b_s, a_w_out, b_w_in, b_conv_w, b_w_out, ln1_g, ln1_b, ln2_g, ln2_b, moe_w_group, moe_b_group, moe_w_expert, moe_b_expert, moe_w_gate_up, moe_w_down, ple_w_gate, ple_b_gate, ple_w_proj):
    xs = (x_prompt.reshape(N_PROMPT_TOK, D_MODEL), x_sample.reshape(N_SAMPLE_TOK, D_MODEL))
    pp = p_prompt.reshape(DEPTH, N_PROMPT_TOK, PLE_DIM)
    ps = p_sample.reshape(DEPTH, N_SAMPLE_TOK, PLE_DIM)

    reps = CHUNK // DEC_SEQ
    ws_sample = jnp.tile(a_w_s[:, :, :DEC_SEQ, :DEC_SEQ], (1, 1, reps, reps))
    ws2 = jnp.stack([a_w_s, ws_sample], axis=1)
    bs_sample = jnp.tile(a_b_s[:, :, :DEC_SEQ], (1, 1, reps))
    bs2 = _pad_lanes(jnp.swapaxes(jnp.stack([a_b_s, bs_sample], axis=1), -1, -2), V7X_LANES)

    zeros = jnp.zeros((state_conv.shape[0], DEC_BATCH, DEC_SEQ, B_WIDTH), _F32)
    past1 = zeros.at[:, :, 0].set(state_conv[:, :, 1]).reshape(-1, N_SAMPLE_TOK, B_WIDTH)
    past2 = (zeros.at[:, :, 0].set(state_conv[:, :, 0]).at[:, :, 1].set(state_conv[:, :, 1])
             .reshape(-1, N_SAMPLE_TOK, B_WIDTH))

    w_router = _pad_lanes(jnp.concatenate([moe_w_group, moe_w_expert], axis=-1), ROUTER_LANES).astype(_BF16)
    b_router = _pad_lanes(jnp.concatenate([moe_b_group, moe_b_expert], axis=-1), ROUTER_LANES)

    v_prompt, v_sample, st_prompt, st_sample = [], [], [], []
    for i in range(DEPTH):
        j = i // 2
        if i % 2 == 0:
            x1, vp, vs = _mixer_a(xs, a_w_in[j].astype(_BF16), _row(a_b_in[j]), _row(a_ln_g[j]),
                                  _row(a_ln_b[j]), ws2[j], bs2[j], a_w_out[j].astype(_BF16),
                                  _row(ln1_g[i]), _row(ln1_b[i]))
            v_prompt.append(vp)
            v_sample.append(vs)
        else:
            x1, stp, sts = _mixer_b(xs[0], b_w_in[j].astype(_BF16), _pad_rows(b_conv_w[j]),
                                    b_w_out[j].astype(_BF16), _row(ln1_g[i]), _row(ln1_b[i]), past1[j], past2[j])
            st_prompt.append(stp)
            st_sample.append(sts)
        xs = _moe_ple(i, i == DEPTH - 1, x1, w_router[i], _row(b_router[i]), moe_w_gate_up[i].astype(_BF16),
                      moe_w_down[i].astype(_BF16), _row(ln2_g[i]), _row(ln2_b[i]), ple_w_gate[i].astype(_BF16), _row(ple_b_gate[i]),
                      ple_w_proj[i].astype(_BF16), pp, ps)

    keep = CONV_WIDTH - 1
    return (xs[0].reshape(BATCH, SEQ, D_MODEL),
            xs[1].reshape(DEC_BATCH, DEC_SEQ, D_MODEL),
            jnp.stack(v_prompt).reshape(-1, BATCH, CHUNK, A_WIDTH),
            jnp.stack(v_sample).reshape(-1, DEC_BATCH, DEC_SEQ, A_WIDTH),
            jnp.stack(st_prompt).reshape(-1, BATCH, V7X_SUBLANES, B_WIDTH)[:, :, V7X_SUBLANES - keep:],
            jnp.stack(st_sample).reshape(-1, DEC_BATCH, DEC_SEQ, B_WIDTH)[:, :, DEC_SEQ - keep:])
```

```python
import functools
import math

import jax
import jax.numpy as jnp
from jax import lax
from jax.experimental import pallas as pl
from jax.experimental.pallas import tpu as pltpu

D_MODEL = 1024
BATCH = 8
SEQ = 2048
DEPTH = 4
DEC_BATCH = 128
DEC_SEQ = 8
CHUNK = 128
A_HEADS = 4
A_WIDTH = 3 * D_MODEL
A_HEAD_DIM = A_WIDTH // A_HEADS
CONV_WIDTH = 3
B_WIDTH = D_MODEL
N_GROUPS = 4
EXPERTS_PER_GROUP = 4
N_EXPERTS = N_GROUPS * EXPERTS_PER_GROUP
D_EXPERT = D_MODEL // 4
PLE_DIM = 256
LN_EPS = 1e-5
DEEPNORM_ALPHA = (2 * DEPTH) ** 0.25

V7X_LANES = 128
V7X_SUBLANES = 8
V7X_VMEM_BYTES = 64 * 1024 * 1024

N_PROMPT_TOK = BATCH * SEQ
N_SAMPLE_TOK = DEC_BATCH * DEC_SEQ
N_TOK = N_PROMPT_TOK + N_SAMPLE_TOK
TM = 256
N_PROMPT_BLOCKS = N_PROMPT_TOK // TM
N_SAMPLE_BLOCKS = N_SAMPLE_TOK // TM
N_BLOCKS = N_PROMPT_BLOCKS + N_SAMPLE_BLOCKS
BLOCKS_PER_SEQ = SEQ // TM
ROUTER_ROWS = 32
VMEM_LIMIT = V7X_VMEM_BYTES - 4 * 1024 * 1024

MOE_TM = 512
SUB = 256
N_SUB = MOE_TM // SUB
SLOTS = 64
SUB_SORTED = N_EXPERTS * SLOTS
N_PROMPT_MOE_BLOCKS = N_PROMPT_TOK // MOE_TM

assert N_PROMPT_TOK % MOE_TM == 0 and N_SAMPLE_TOK % MOE_TM == 0 and MOE_TM % SUB == 0
assert SEQ % TM == 0 and N_SAMPLE_TOK % TM == 0 and TM % CHUNK == 0 and TM % DEC_SEQ == 0
assert CHUNK % DEC_SEQ == 0 and CONV_WIDTH - 1 <= V7X_SUBLANES

_BF16 = jnp.bfloat16
_F32 = jnp.float32
_I32 = jnp.int32


def _layer_norm(x, g, b):
    mu = jnp.mean(x, axis=-1, keepdims=True)
    xc = x - mu
    var = jnp.mean(xc * xc, axis=-1, keepdims=True)
    return xc * lax.rsqrt(var + LN_EPS) * g + b


def _gelu(x):
    return 0.5 * x * (1.0 + lax.erf(x * math.sqrt(0.5)))


def _dot(a, b):
    return jnp.dot(a, b, preferred_element_type=_F32)


def _const_spec(shape):
    zeros = (0,) * len(shape)
    return pl.BlockSpec(shape, lambda i: zeros, pipeline_mode=pl.Buffered(1))


def _tok_spec(width):
    return pl.BlockSpec((TM, width), lambda i: (i, 0))


def _prompt_block(i):
    return jnp.minimum(i, N_PROMPT_BLOCKS - 1)


def _sample_block(i):
    return jnp.maximum(i - N_PROMPT_BLOCKS, 0)


def _prompt_spec(width):
    return pl.BlockSpec((TM, width), lambda i: (_prompt_block(i), 0))


def _sample_spec(width):
    return pl.BlockSpec((TM, width), lambda i: (_sample_block(i), 0))


def _prompt_state_spec(rows, width):
    return pl.BlockSpec((rows, width), lambda i: (_prompt_block(i) // BLOCKS_PER_SEQ, 0))


_PARAMS = pltpu.CompilerParams(dimension_semantics=("arbitrary",), vmem_limit_bytes=VMEM_LIMIT)


def _is_sample():
    return pl.program_id(0) >= N_PROMPT_BLOCKS


def _read_tokens(split, refs):
    if not split:
        return refs[0][...]
    return jnp.where(_is_sample(), refs[1][...], refs[0][...])


def _write_by_side(prompt_ref, sample_ref, prompt_val, sample_val):
    @pl.when(jnp.logical_not(_is_sample()))
    def _():
        prompt_ref[...] = prompt_val

    @pl.when(_is_sample())
    def _():
        sample_ref[...] = sample_val


def _mixer_a_kernel(split_in, *refs):
    n_x = 2 if split_in else 1
    x_refs = refs[:n_x]
    (w_in_ref, b_in_ref, lng_ref, lnb_ref, ws_ref, bs_ref, w_out_ref, g1_ref, b1_ref,
     x1_ref, vp_ref, vs_ref, t_ref) = refs[n_x:]
    r = lax.broadcasted_iota(_I32, (CHUNK, CHUNK), 0)
    c = lax.broadcasted_iota(_I32, (CHUNK, CHUNK), 1)
    shift = jnp.where(_is_sample(), int(math.log2(DEC_SEQ)), int(math.log2(CHUNK)))
    mask = jnp.logical_and(r >= c, (r >> shift) == (c >> shift))
    w_spatial = [jnp.where(mask, ws_ref[0, h], 0.0).astype(_BF16) for h in range(A_HEADS)]

    for k in range(TM // CHUNK):
        rows = slice(k * CHUNK, (k + 1) * CHUNK)
        if split_in:
            x = jnp.where(_is_sample(), x_refs[1][rows, :], x_refs[0][rows, :])
        else:
            x = x_refs[0][rows, :]
        xb = x.astype(_BF16)
        u = _gelu(_dot(xb, w_in_ref[:, :A_WIDTH]) + b_in_ref[:, :A_WIDTH])
        v = _gelu(_dot(xb, w_in_ref[:, A_WIDTH:]) + b_in_ref[:, A_WIDTH:])
        v = _layer_norm(v, lng_ref[...], lnb_ref[...])
        vs_ref[rows, :] = v
        vb = v.astype(_BF16)
        for h in range(A_HEADS):
            cols = slice(h * A_HEAD_DIM, (h + 1) * A_HEAD_DIM)
            sg = _dot(w_spatial[h], vb[:, cols]) + bs_ref[0, :, h:h + 1]
            t_ref[rows, cols] = (u[:, cols] * sg).astype(_BF16)
        y = _dot(t_ref[rows, :], w_out_ref[...])
        x1_ref[rows, :] = _layer_norm(DEEPNORM_ALPHA * x + y, g1_ref[...], b1_ref[...])

    @pl.when(jnp.logical_not(_is_sample()))
    def _():
        vp_ref[...] = vs_ref[TM - CHUNK:, :]


def _mixer_a(xs, w_in, b_in, ln_g, ln_b, ws2, bs2, w_out, g1, b1):
    split_in = len(xs) == 2
    x_specs = [_prompt_spec(D_MODEL), _sample_spec(D_MODEL)] if split_in else [_tok_spec(D_MODEL)]
    side = lambda i: jnp.where(i < N_PROMPT_BLOCKS, 0, 1)
    return pl.pallas_call(
        functools.partial(_mixer_a_kernel, split_in),
        grid=(N_BLOCKS,),
        in_specs=x_specs + [
            _const_spec((D_MODEL, 2 * A_WIDTH)),
            _const_spec((1, 2 * A_WIDTH)),
            _const_spec((1, A_WIDTH)),
            _const_spec((1, A_WIDTH)),
            pl.BlockSpec((1, A_HEADS, CHUNK, CHUNK), lambda i: (side(i), 0, 0, 0)),
            pl.BlockSpec((1, CHUNK, V7X_LANES), lambda i: (side(i), 0, 0)),
            _const_spec((A_WIDTH, D_MODEL)),
            _const_spec((1, D_MODEL)),
            _const_spec((1, D_MODEL)),
        ],
        out_specs=[
            _tok_spec(D_MODEL),
            _prompt_state_spec(CHUNK, A_WIDTH),
            _sample_spec(A_WIDTH),
        ],
        out_shape=[
            jax.ShapeDtypeStruct((N_TOK, D_MODEL), _F32),
            jax.ShapeDtypeStruct((BATCH * CHUNK, A_WIDTH), _F32),
            jax.ShapeDtypeStruct((N_SAMPLE_TOK, A_WIDTH), _F32),
        ],
        scratch_shapes=[pltpu.VMEM((TM, A_WIDTH), _BF16)],
        compiler_params=_PARAMS,
        name="mixer_a",
    )(*xs, w_in, b_in, ln_g, ln_b, ws2, bs2, w_out, g1, b1)


def _mixer_b_kernel(x_ref, w_in_ref, cw_ref, w_out_ref, g1_ref, b1_ref, past1_ref, past2_ref,
                    x1_ref, stp_ref, sts_ref, ext_ref):
    i = pl.program_id(0)
    is_sample = _is_sample()
    @pl.when(jnp.logical_or(is_sample, i % BLOCKS_PER_SEQ == 0))
    def _():
        ext_ref[0:V7X_SUBLANES, :] = jnp.zeros((V7X_SUBLANES, B_WIDTH), _F32)

    pos = lax.broadcasted_iota(_I32, (CHUNK, B_WIDTH), 0) % DEC_SEQ
    pos = jnp.where(is_sample, pos, DEC_SEQ)
    for k in range(TM // CHUNK):
        rows = slice(k * CHUNK, (k + 1) * CHUNK)
        first = V7X_SUBLANES + k * CHUNK
        x = x_ref[rows, :]
        h = _dot(x.astype(_BF16), w_in_ref[...])
        gate_b = h[:, :B_WIDTH]
        p = h[:, B_WIDTH:2 * B_WIDTH] * h[:, 2 * B_WIDTH:]
        sts_ref[rows, :] = p
        ext_ref[first:first + CHUNK, :] = p
        prev1 = jnp.where(pos < 1, past1_ref[rows, :], ext_ref[first - 1:first - 1 + CHUNK, :])
        prev2 = jnp.where(pos < 2, past2_ref[rows, :], ext_ref[first - 2:first - 2 + CHUNK, :])
        conv = cw_ref[0:1, :] * prev2 + cw_ref[1:2, :] * prev1 + cw_ref[2:3, :] * p
        y = _dot((gate_b * conv).astype(_BF16), w_out_ref[...])
        x1_ref[rows, :] = _layer_norm(DEEPNORM_ALPHA * x + y, g1_ref[...], b1_ref[...])
    ext_ref[0:V7X_SUBLANES, :] = ext_ref[TM:TM + V7X_SUBLANES, :]

    @pl.when(jnp.logical_not(is_sample))
    def _():
        stp_ref[...] = ext_ref[TM:TM + V7X_SUBLANES, :]


def _mixer_b(x, w_in, conv_w, w_out, g1, b1, past1, past2):
    return pl.pallas_call(
        _mixer_b_kernel,
        grid=(N_BLOCKS,),
        in_specs=[
            _tok_spec(D_MODEL),
            _const_spec((D_MODEL, 3 * B_WIDTH)),
            _const_spec((V7X_SUBLANES, B_WIDTH)),
            _const_spec((B_WIDTH, D_MODEL)),
            _const_spec((1, D_MODEL)),
            _const_spec((1, D_MODEL)),
            _sample_spec(B_WIDTH),
            _sample_spec(B_WIDTH),
        ],
        out_specs=[
            _tok_spec(D_MODEL),
            _prompt_state_spec(V7X_SUBLANES, B_WIDTH),
            _sample_spec(B_WIDTH),
        ],
        out_shape=[
            jax.ShapeDtypeStruct((N_TOK, D_MODEL), _F32),
            jax.ShapeDtypeStruct((BATCH * V7X_SUBLANES, B_WIDTH), _F32),
            jax.ShapeDtypeStruct((N_SAMPLE_TOK, B_WIDTH), _F32),
        ],
        scratch_shapes=[pltpu.VMEM((V7X_SUBLANES + TM, B_WIDTH), _F32)],
        compiler_params=_PARAMS,
        name="mixer_b",
    )(x, w_in, conv_w, w_out, g1, b1, past1, past2)


def _route(logits):
    row = lax.broadcasted_iota(_I32, logits.shape, 0)
    row_f = row.astype(_F32)
    neg = jnp.float32(-jnp.inf)
    far = jnp.float32(ROUTER_ROWS)

    def first_argmax(vals):
        top = jnp.max(vals, axis=0, keepdims=True)
        idx = jnp.min(jnp.where(vals == top, row_f, far), axis=0, keepdims=True)
        return top, idx

    is_group = row < N_GROUPS
    g_top, g_idx = first_argmax(jnp.where(is_group, logits, neg))
    g_w = 1.0 / jnp.sum(jnp.where(is_group, jnp.exp(logits - g_top), 0.0), axis=0, keepdims=True)
    lo = N_GROUPS + EXPERTS_PER_GROUP * g_idx
    in_group = jnp.logical_and(row_f >= lo, row_f < lo + EXPERTS_PER_GROUP)
    e_logits = jnp.where(in_group, logits, neg)
    v1, i1 = first_argmax(e_logits)
    v2, i2 = first_argmax(jnp.where(row_f == i1, neg, e_logits))
    e2 = jnp.exp(v2 - v1)
    return i1, i2, g_w / (1.0 + e2), g_w * e2 / (1.0 + e2)


def _expert_act(hgu, scale):
    gate = hgu[:, :D_EXPERT]
    return (gate * jax.nn.sigmoid(gate) * hgu[:, D_EXPERT:] * scale).astype(_BF16)


def _moe_ple_kernel(split_out, x1_ref, wr_ref, br_ref, wgu_ref, wd_ref, g2_ref, b2_ref, wg_ref, bg_ref,
                    wp_ref, pp_ref, ps_ref, *rest):
    n_out = 2 if split_out else 1
    out_refs = rest[:n_out]
    xs_ref, ws_ref, perm_ref, ylo_ref, flag_v, flag_s, sem = rest[n_out:]
    yhi_ref = xs_ref
    is_sample = pl.program_id(0) >= N_PROMPT_MOE_BLOCKS
    xb = x1_ref[...].astype(_BF16)
    logits = lax.dot_general(wr_ref[...], xb, (((1,), (1,)), ((), ())), preferred_element_type=_F32)
    i1, i2, w1, w2 = _route(logits + br_ref[...])

    expert_row = lax.broadcasted_iota(_I32, (N_EXPERTS, SUB), 0).astype(_F32) + N_GROUPS
    r = lax.broadcasted_iota(_I32, (SUB, SUB), 0)
    c = lax.broadcasted_iota(_I32, (SUB, SUB), 1)
    earlier = jnp.where(r < c, 1.0, 0.0).astype(_BF16)
    sorted_row = lax.broadcasted_iota(_I32, (SUB_SORTED, SUB), 0).astype(_F32)
    fullest = jnp.zeros((1, 1), _F32)
    for s in range(N_SUB):
        rows = slice(s * SUB, (s + 1) * SUB)
        hit1 = expert_row == i1[:, rows]
        hit2 = expert_row == i2[:, rows]
        onehot = jnp.where(jnp.logical_or(hit1, hit2), 1.0, 0.0)
        fullest = jnp.maximum(fullest, jnp.max(jnp.sum(onehot, axis=1, keepdims=True), axis=0, keepdims=True))
        rank = _dot(onehot.astype(_BF16), earlier)
        q1 = (i1[:, rows] - N_GROUPS) * SLOTS + jnp.sum(jnp.where(hit1, rank, 0.0), axis=0, keepdims=True)
        q2 = (i2[:, rows] - N_GROUPS) * SLOTS + jnp.sum(jnp.where(hit2, rank, 0.0), axis=0, keepdims=True)
        at1 = sorted_row == q1
        at2 = sorted_row == q2
        perm = jnp.where(jnp.logical_or(at1, at2), 1.0, 0.0).astype(_BF16)
        perm_ref[s] = perm
        xs_ref[s] = _dot(perm, xb[rows]).astype(_BF16)
        weight = jnp.sum(jnp.where(at1, w1[:, rows], 0.0) + jnp.where(at2, w2[:, rows], 0.0),
                         axis=-1, keepdims=True)
        ws_ref[s] = jnp.broadcast_to(weight, (SUB_SORTED, V7X_LANES))

    flag_v[...] = jnp.broadcast_to(fullest, flag_v.shape).astype(_I32)
    to_smem = pltpu.make_async_copy(flag_v, flag_s, sem)
    to_smem.start()
    to_smem.wait()
    fits = flag_s[0, 0] <= SLOTS

    def finish(rows, moe):
        x2 = _layer_norm(DEEPNORM_ALPHA * x1_ref[rows, :] + moe, g2_ref[...], b2_ref[...])
        gate = jax.nn.sigmoid(_dot(x2.astype(_BF16), wg_ref[...]) + bg_ref[...])
        p = jnp.where(is_sample, ps_ref[0, rows, :], pp_ref[0, rows, :])
        out = x2 + gate * _dot(p.astype(_BF16), wp_ref[...])
        if split_out:
            @pl.when(jnp.logical_not(is_sample))
            def _():
                out_refs[0][rows, :] = out

            @pl.when(is_sample)
            def _():
                out_refs[1][rows, :] = out
        else:
            out_refs[0][rows, :] = out

    @pl.when(fits)
    def _():
        for e in range(N_EXPERTS):
            slots = slice(e * SLOTS, (e + 1) * SLOTS)
            xt = jnp.concatenate([xs_ref[s, slots, :] for s in range(N_SUB)], axis=0)
            wt = jnp.concatenate([ws_ref[s, slots, 0:1] for s in range(N_SUB)], axis=0)
            y = _dot(_expert_act(_dot(xt, wgu_ref[e]), wt), wd_ref[e])
            hi = y.astype(_BF16)
            lo = (y - hi.astype(_F32)).astype(_BF16)
            for s in range(N_SUB):
                yhi_ref[s, slots, :] = hi[s * SLOTS:(s + 1) * SLOTS, :]
                ylo_ref[s, slots, :] = lo[s * SLOTS:(s + 1) * SLOTS, :]
        for s in range(N_SUB):
            back = lambda y: lax.dot_general(perm_ref[s], y, (((0,), (0,)), ((), ())), preferred_element_type=_F32)
            finish(slice(s * SUB, (s + 1) * SUB), back(yhi_ref[s]) + back(ylo_ref[s]))

    @pl.when(jnp.logical_not(fits))
    def _():
        row_all = lax.broadcasted_iota(_I32, (V7X_LANES, MOE_TM), 0).astype(_F32)
        combine = (jnp.where(row_all == i1, w1, 0.0) + jnp.where(row_all == i2, w2, 0.0)).T
        acc = jnp.zeros((MOE_TM, D_MODEL), _F32)
        for e in range(N_EXPERTS):
            w = combine[:, N_GROUPS + e:N_GROUPS + e + 1]
            acc = acc + _dot(_expert_act(_dot(xb, wgu_ref[e]), w), wd_ref[e])
        finish(slice(0, MOE_TM), acc)


def _moe_ple(layer, split_out, x1, wr, br, wgu, wd, g2, b2, wg, bg, wp, p_prompt, p_sample):
    prompt_block = lambda i: jnp.minimum(i, N_PROMPT_MOE_BLOCKS - 1)
    sample_block = lambda i: jnp.maximum(i - N_PROMPT_MOE_BLOCKS, 0)
    if split_out:
        out_specs = [pl.BlockSpec((MOE_TM, D_MODEL), lambda i: (prompt_block(i), 0)),
                     pl.BlockSpec((MOE_TM, D_MODEL), lambda i: (sample_block(i), 0))]
        out_shape = [jax.ShapeDtypeStruct((N_PROMPT_TOK, D_MODEL), _F32),
                     jax.ShapeDtypeStruct((N_SAMPLE_TOK, D_MODEL), _F32)]
    else:
        out_specs = [pl.BlockSpec((MOE_TM, D_MODEL), lambda i: (i, 0))]
        out_shape = [jax.ShapeDtypeStruct((N_TOK, D_MODEL), _F32)]
    return pl.pallas_call(
        functools.partial(_moe_ple_kernel, split_out),
        grid=(N_TOK // MOE_TM,),
        in_specs=[
            pl.BlockSpec((MOE_TM, D_MODEL), lambda i: (i, 0)),
            _const_spec((ROUTER_ROWS, D_MODEL)),
            _const_spec((ROUTER_ROWS, 1)),
            _const_spec((N_EXPERTS, D_MODEL, 2 * D_EXPERT)),
            _const_spec((N_EXPERTS, D_EXPERT, D_MODEL)),
            _const_spec((1, D_MODEL)),
            _const_spec((1, D_MODEL)),
            _const_spec((D_MODEL, D_MODEL)),
            _const_spec((1, D_MODEL)),
            _const_spec((PLE_DIM, D_MODEL)),
            pl.BlockSpec((1, MOE_TM, PLE_DIM), lambda i: (layer, prompt_block(i), 0)),
            pl.BlockSpec((1, MOE_TM, PLE_DIM), lambda i: (layer, sample_block(i), 0)),
        ],
        out_specs=out_specs,
        out_shape=out_shape,
        scratch_shapes=[
            pltpu.VMEM((N_SUB, SUB_SORTED, D_MODEL), _BF16),
            pltpu.VMEM((N_SUB, SUB_SORTED, V7X_LANES), _F32),
            pltpu.VMEM((N_SUB, SUB_SORTED, SUB), _BF16),
            pltpu.VMEM((N_SUB, SUB_SORTED, D_MODEL), _BF16),
            pltpu.VMEM((V7X_SUBLANES, V7X_LANES), _I32),
            pltpu.SMEM((V7X_SUBLANES, V7X_LANES), _I32),
            pltpu.SemaphoreType.DMA,
        ],
        compiler_params=_PARAMS,
        name="moe_ple",
    )(x1, wr, br, wgu, wd, g2, b2, wg, bg, wp, p_prompt, p_sample)


def _row(v):
    return v.reshape(1, -1)


def _pad_lanes(a, lanes):
    return jnp.pad(a, [(0, 0)] * (a.ndim - 1) + [(0, lanes - a.shape[-1])])


def _pad_rows(a):
    return jnp.pad(a, [(0, V7X_SUBLANES - a.shape[0]), (0, 0)])


def kernel(x_prompt, x_sample, state_conv, p_prompt, p_sample, a_w_in, a_b_in, a_ln_g, a_ln_b, a_w_s, a_b_s, a_w_out, b_w_in, b_conv_w, b_w_out, ln1_g, ln1_b, ln2_g, ln2_b, moe_w_group, moe_b_group, moe_w_expert, moe_b_expert, moe_w_gate_up, moe_w_down, ple_w_gate, ple_b_gate, ple_w_proj):
    xs = (x_prompt.reshape(N_PROMPT_TOK, D_MODEL), x_sample.reshape(N_SAMPLE_TOK, D_MODEL))
    pp = p_prompt.reshape(DEPTH, N_PROMPT_TOK, PLE_DIM)
    ps = p_sample.reshape(DEPTH, N_SAMPLE_TOK, PLE_DIM)

    reps = CHUNK // DEC_SEQ
    ws_sample = jnp.tile(a_w_s[:, :, :DEC_SEQ, :DEC_SEQ], (1, 1, reps, reps))
    ws2 = jnp.stack([a_w_s, ws_sample], axis=1)
    bs_sample = jnp.tile(a_b_s[:, :, :DEC_SEQ], (1, 1, reps))
    bs2 = _pad_lanes(jnp.swapaxes(jnp.stack([a_b_s, bs_sample], axis=1), -1, -2), V7X_LANES)

    zeros = jnp.zeros((state_conv.shape[0], DEC_BATCH, DEC_SEQ, B_WIDTH), _F32)
    past1 = zeros.at[:, :, 0].set(state_conv[:, :, 1]).reshape(-1, N_SAMPLE_TOK, B_WIDTH)
    past2 = (zeros.at[:, :, 0].set(state_conv[:, :, 0]).at[:, :, 1].set(state_conv[:, :, 1])
             .reshape(-1, N_SAMPLE_TOK, B_WIDTH))

    w_router = jnp.swapaxes(_pad_lanes(jnp.concatenate([moe_w_group, moe_w_expert], axis=-1), ROUTER_ROWS),
                            -1, -2).astype(_BF16)
    b_router = _pad_lanes(jnp.concatenate([moe_b_group, moe_b_expert], axis=-1), ROUTER_ROWS)[..., None]

    v_prompt, v_sample, st_prompt, st_sample = [], [], [], []
    for i in range(DEPTH):
        j = i // 2
        if i % 2 == 0:
            x1, vp, vs = _mixer_a(xs, a_w_in[j].astype(_BF16), _row(a_b_in[j]), _row(a_ln_g[j]),
                                  _row(a_ln_b[j]), ws2[j], bs2[j], a_w_out[j].astype(_BF16),
                                  _row(ln1_g[i]), _row(ln1_b[i]))
            v_prompt.append(vp)
            v_sample.append(vs)
        else:
            x1, stp, sts = _mixer_b(xs[0], b_w_in[j].astype(_BF16), _pad_rows(b_conv_w[j]),
                                    b_w_out[j].astype(_BF16), _row(ln1_g[i]), _row(ln1_b[i]), past1[j], past2[j])
            st_prompt.append(stp)
            st_sample.append(sts)
        xs = _moe_ple(i, i == DEPTH - 1, x1, w_router[i], b_router[i], moe_w_gate_up[i].astype(_BF16),
                      moe_w_down[i].astype(_BF16), _row(ln2_g[i]), _row(ln2_b[i]), ple_w_gate[i].astype(_BF16), _row(ple_b_gate[i]),
                      ple_w_proj[i].astype(_BF16), pp, ps)

    keep = CONV_WIDTH - 1
    return (xs[0].reshape(BATCH, SEQ, D_MODEL),
            xs[1].reshape(DEC_BATCH, DEC_SEQ, D_MODEL),
            jnp.stack(v_prompt).reshape(-1, BATCH, CHUNK, A_WIDTH),
            jnp.stack(v_sample).reshape(-1, DEC_BATCH, DEC_SEQ, A_WIDTH),
            jnp.stack(st_prompt).reshape(-1, BATCH, V7X_SUBLANES, B_WIDTH)[:, :, V7X_SUBLANES - keep:],
            jnp.stack(st_sample).reshape(-1, DEC_BATCH, DEC_SEQ, B_WIDTH)[:, :, DEC_SEQ - keep:])
```

```python
import functools
import math

import jax
import jax.numpy as jnp
from jax import lax
from jax.experimental import pallas as pl
from jax.experimental.pallas import tpu as pltpu

D_MODEL = 1024
BATCH = 8
SEQ = 2048
DEPTH = 4
DEC_BATCH = 128
DEC_SEQ = 8
CHUNK = 128
A_HEADS = 4
A_WIDTH = 3 * D_MODEL
A_HEAD_DIM = A_WIDTH // A_HEADS
CONV_WIDTH = 3
B_WIDTH = D_MODEL
N_GROUPS = 4
EXPERTS_PER_GROUP = 4
N_EXPERTS = N_GROUPS * EXPERTS_PER_GROUP
D_EXPERT = D_MODEL // 4
PLE_DIM = 256
LN_EPS = 1e-5
DEEPNORM_ALPHA = (2 * DEPTH) ** 0.25

V7X_LANES = 128
V7X_SUBLANES = 8
V7X_VMEM_BYTES = 64 * 1024 * 1024

N_PROMPT_TOK = BATCH * SEQ
N_SAMPLE_TOK = DEC_BATCH * DEC_SEQ
N_TOK = N_PROMPT_TOK + N_SAMPLE_TOK
TM = 256
N_PROMPT_BLOCKS = N_PROMPT_TOK // TM
N_SAMPLE_BLOCKS = N_SAMPLE_TOK // TM
N_BLOCKS = N_PROMPT_BLOCKS + N_SAMPLE_BLOCKS
BLOCKS_PER_SEQ = SEQ // TM
ROUTER_ROWS = 32
VMEM_LIMIT = V7X_VMEM_BYTES - 4 * 1024 * 1024

MOE_TM = 512
SUB = 256
N_SUB = MOE_TM // SUB
SLOTS = 64
SUB_SORTED = N_EXPERTS * SLOTS
N_PROMPT_MOE_BLOCKS = N_PROMPT_TOK // MOE_TM

assert N_PROMPT_TOK % MOE_TM == 0 and N_SAMPLE_TOK % MOE_TM == 0 and MOE_TM % SUB == 0
assert SEQ % TM == 0 and N_SAMPLE_TOK % TM == 0 and TM % CHUNK == 0 and TM % DEC_SEQ == 0
assert CHUNK % DEC_SEQ == 0 and CONV_WIDTH - 1 <= V7X_SUBLANES

_BF16 = jnp.bfloat16
_F32 = jnp.float32
_I32 = jnp.int32


def _layer_norm(x, g, b):
    mu = jnp.mean(x, axis=-1, keepdims=True)
    xc = x - mu
    var = jnp.mean(xc * xc, axis=-1, keepdims=True)
    return xc * lax.rsqrt(var + LN_EPS) * g + b


def _gelu(x):
    return 0.5 * x * (1.0 + lax.erf(x * math.sqrt(0.5)))


def _dot(a, b):
    return jnp.dot(a, b, preferred_element_type=_F32)


def _const_spec(shape):
    zeros = (0,) * len(shape)
    return pl.BlockSpec(shape, lambda i: zeros, pipeline_mode=pl.Buffered(1))


def _tok_spec(width):
    return pl.BlockSpec((TM, width), lambda i: (i, 0))


def _prompt_block(i):
    return jnp.minimum(i, N_PROMPT_BLOCKS - 1)


def _sample_block(i):
    return jnp.maximum(i - N_PROMPT_BLOCKS, 0)


def _prompt_spec(width):
    return pl.BlockSpec((TM, width), lambda i: (_prompt_block(i), 0))


def _sample_spec(width):
    return pl.BlockSpec((TM, width), lambda i: (_sample_block(i), 0))


def _prompt_state_spec(rows, width):
    return pl.BlockSpec((rows, width), lambda i: (_prompt_block(i) // BLOCKS_PER_SEQ, 0))


_PARAMS = pltpu.CompilerParams(dimension_semantics=("arbitrary",), vmem_limit_bytes=VMEM_LIMIT)


def _is_sample():
    return pl.program_id(0) >= N_PROMPT_BLOCKS


def _read_tokens(split, refs):
    if not split:
        return refs[0][...]
    return jnp.where(_is_sample(), refs[1][...], refs[0][...])


def _write_by_side(prompt_ref, sample_ref, prompt_val, sample_val):
    @pl.when(jnp.logical_not(_is_sample()))
    def _():
        prompt_ref[...] = prompt_val

    @pl.when(_is_sample())
    def _():
        sample_ref[...] = sample_val


def _mixer_a_kernel(split_in, *refs):
    n_x = 2 if split_in else 1
    x_refs = refs[:n_x]
    (w_in_ref, b_in_ref, lng_ref, lnb_ref, ws_ref, bs_ref, w_out_ref, g1_ref, b1_ref,
     x1_ref, vp_ref, vs_ref, t_ref) = refs[n_x:]
    r = lax.broadcasted_iota(_I32, (CHUNK, CHUNK), 0)
    c = lax.broadcasted_iota(_I32, (CHUNK, CHUNK), 1)
    shift = jnp.where(_is_sample(), int(math.log2(DEC_SEQ)), int(math.log2(CHUNK)))
    mask = jnp.logical_and(r >= c, (r >> shift) == (c >> shift))
    w_spatial = [jnp.where(mask, ws_ref[0, h], 0.0).astype(_BF16) for h in range(A_HEADS)]

    for k in range(TM // CHUNK):
        rows = slice(k * CHUNK, (k + 1) * CHUNK)
        if split_in:
            x = jnp.where(_is_sample(), x_refs[1][rows, :], x_refs[0][rows, :])
        else:
            x = x_refs[0][rows, :]
        xb = x.astype(_BF16)
        u = _gelu(_dot(xb, w_in_ref[:, :A_WIDTH]) + b_in_ref[:, :A_WIDTH])
        v = _gelu(_dot(xb, w_in_ref[:, A_WIDTH:]) + b_in_ref[:, A_WIDTH:])
        v = _layer_norm(v, lng_ref[...], lnb_ref[...])
        vs_ref[rows, :] = v
        vb = v.astype(_BF16)
        for h in range(A_HEADS):
            cols = slice(h * A_HEAD_DIM, (h + 1) * A_HEAD_DIM)
            sg = _dot(w_spatial[h], vb[:, cols]) + bs_ref[0, :, h:h + 1]
            t_ref[rows, cols] = (u[:, cols] * sg).astype(_BF16)
        y = _dot(t_ref[rows, :], w_out_ref[...])
        x1_ref[rows, :] = _layer_norm(DEEPNORM_ALPHA * x + y, g1_ref[...], b1_ref[...])

    @pl.when(jnp.logical_not(_is_sample()))
    def _():
        vp_ref[...] = vs_ref[TM - CHUNK:, :]


def _mixer_a(xs, w_in, b_in, ln_g, ln_b, ws2, bs2, w_out, g1, b1):
    split_in = len(xs) == 2
    x_specs = [_prompt_spec(D_MODEL), _sample_spec(D_MODEL)] if split_in else [_tok_spec(D_MODEL)]
    side = lambda i: jnp.where(i < N_PROMPT_BLOCKS, 0, 1)
    return pl.pallas_call(
        functools.partial(_mixer_a_kernel, split_in),
        grid=(N_BLOCKS,),
        in_specs=x_specs + [
            _const_spec((D_MODEL, 2 * A_WIDTH)),
            _const_spec((1, 2 * A_WIDTH)),
            _const_spec((1, A_WIDTH)),
            _const_spec((1, A_WIDTH)),
            pl.BlockSpec((1, A_HEADS, CHUNK, CHUNK), lambda i: (side(i), 0, 0, 0)),
            pl.BlockSpec((1, CHUNK, V7X_LANES), lambda i: (side(i), 0, 0)),
            _const_spec((A_WIDTH, D_MODEL)),
            _const_spec((1, D_MODEL)),
            _const_spec((1, D_MODEL)),
        ],
        out_specs=[
            _tok_spec(D_MODEL),
            _prompt_state_spec(CHUNK, A_WIDTH),
            _sample_spec(A_WIDTH),
        ],
        out_shape=[
            jax.ShapeDtypeStruct((N_TOK, D_MODEL), _F32),
            jax.ShapeDtypeStruct((BATCH * CHUNK, A_WIDTH), _F32),
            jax.ShapeDtypeStruct((N_SAMPLE_TOK, A_WIDTH), _F32),
        ],
        scratch_shapes=[pltpu.VMEM((TM, A_WIDTH), _BF16)],
        compiler_params=_PARAMS,
        name="mixer_a",
    )(*xs, w_in, b_in, ln_g, ln_b, ws2, bs2, w_out, g1, b1)


def _mixer_b_kernel(x_ref, w_in_ref, cw_ref, w_out_ref, g1_ref, b1_ref, past1_ref, past2_ref,
                    x1_ref, stp_ref, sts_ref, ext_ref):
    i = pl.program_id(0)
    is_sample = _is_sample()
    @pl.when(jnp.logical_or(is_sample, i % BLOCKS_PER_SEQ == 0))
    def _():
        ext_ref[0:V7X_SUBLANES, :] = jnp.zeros((V7X_SUBLANES, B_WIDTH), _F32)

    pos = lax.broadcasted_iota(_I32, (TM, B_WIDTH), 0) % DEC_SEQ
    pos = jnp.where(is_sample, pos, DEC_SEQ)
    x = x_ref[...]
    h = _dot(x.astype(_BF16), w_in_ref[...])
    gate_b = h[:, :B_WIDTH]
    p = h[:, B_WIDTH:2 * B_WIDTH] * h[:, 2 * B_WIDTH:]
    sts_ref[...] = p
    ext_ref[V7X_SUBLANES:, :] = p
    prev1 = jnp.where(pos < 1, past1_ref[...], ext_ref[V7X_SUBLANES - 1:V7X_SUBLANES - 1 + TM, :])
    prev2 = jnp.where(pos < 2, past2_ref[...], ext_ref[V7X_SUBLANES - 2:V7X_SUBLANES - 2 + TM, :])
    conv = cw_ref[0:1, :] * prev2 + cw_ref[1:2, :] * prev1 + cw_ref[2:3, :] * p
    y = _dot((gate_b * conv).astype(_BF16), w_out_ref[...])
    x1_ref[...] = _layer_norm(DEEPNORM_ALPHA * x + y, g1_ref[...], b1_ref[...])
    ext_ref[0:V7X_SUBLANES, :] = ext_ref[TM:TM + V7X_SUBLANES, :]

    @pl.when(jnp.logical_not(is_sample))
    def _():
        stp_ref[...] = ext_ref[TM:TM + V7X_SUBLANES, :]


def _mixer_b(x, w_in, conv_w, w_out, g1, b1, past1, past2):
    return pl.pallas_call(
        _mixer_b_kernel,
        grid=(N_BLOCKS,),
        in_specs=[
            _tok_spec(D_MODEL),
            _const_spec((D_MODEL, 3 * B_WIDTH)),
            _const_spec((V7X_SUBLANES, B_WIDTH)),
            _const_spec((B_WIDTH, D_MODEL)),
            _const_spec((1, D_MODEL)),
            _const_spec((1, D_MODEL)),
            _sample_spec(B_WIDTH),
            _sample_spec(B_WIDTH),
        ],
        out_specs=[
            _tok_spec(D_MODEL),
            _prompt_state_spec(V7X_SUBLANES, B_WIDTH),
            _sample_spec(B_WIDTH),
        ],
        out_shape=[
            jax.ShapeDtypeStruct((N_TOK, D_MODEL), _F32),
            jax.ShapeDtypeStruct((BATCH * V7X_SUBLANES, B_WIDTH), _F32),
            jax.ShapeDtypeStruct((N_SAMPLE_TOK, B_WIDTH), _F32),
        ],
        scratch_shapes=[pltpu.VMEM((V7X_SUBLANES + TM, B_WIDTH), _F32)],
        compiler_params=_PARAMS,
        name="mixer_b",
    )(x, w_in, conv_w, w_out, g1, b1, past1, past2)


def _route(logits):
    row = lax.broadcasted_iota(_I32, logits.shape, 0)
    row_f = row.astype(_F32)
    neg = jnp.float32(-jnp.inf)
    far = jnp.float32(ROUTER_ROWS)

    def first_argmax(vals):
        top = jnp.max(vals, axis=0, keepdims=True)
        idx = jnp.min(jnp.where(vals == top, row_f, far), axis=0, keepdims=True)
        return top, idx

    is_group = row < N_GROUPS
    g_top, g_idx = first_argmax(jnp.where(is_group, logits, neg))
    g_w = 1.0 / jnp.sum(jnp.where(is_group, jnp.exp(logits - g_top), 0.0), axis=0, keepdims=True)
    lo = N_GROUPS + EXPERTS_PER_GROUP * g_idx
    in_group = jnp.logical_and(row_f >= lo, row_f < lo + EXPERTS_PER_GROUP)
    e_logits = jnp.where(in_group, logits, neg)
    v1, i1 = first_argmax(e_logits)
    v2, i2 = first_argmax(jnp.where(row_f == i1, neg, e_logits))
    e2 = jnp.exp(v2 - v1)
    return i1, i2, g_w / (1.0 + e2), g_w * e2 / (1.0 + e2)


def _expert_act(hgu, scale):
    gate = hgu[:, :D_EXPERT]
    return (gate * jax.nn.sigmoid(gate) * hgu[:, D_EXPERT:] * scale).astype(_BF16)


def _moe_ple_kernel(split_out, x1_ref, wr_ref, br_ref, wgu_ref, wd_ref, g2_ref, b2_ref, wg_ref, bg_ref,
                    wp_ref, pp_ref, ps_ref, *rest):
    n_out = 2 if split_out else 1
    out_refs = rest[:n_out]
    xs_ref, ws_ref, perm_ref, flag_v, flag_s, sem = rest[n_out:]
    ys_ref = xs_ref
    is_sample = pl.program_id(0) >= N_PROMPT_MOE_BLOCKS
    xb = x1_ref[...].astype(_BF16)
    logits = lax.dot_general(wr_ref[...], xb, (((1,), (1,)), ((), ())), preferred_element_type=_F32)
    i1, i2, w1, w2 = _route(logits + br_ref[...])

    expert_row = lax.broadcasted_iota(_I32, (N_EXPERTS, SUB), 0).astype(_F32) + N_GROUPS
    r = lax.broadcasted_iota(_I32, (SUB, SUB), 0)
    c = lax.broadcasted_iota(_I32, (SUB, SUB), 1)
    earlier = jnp.where(r < c, 1.0, 0.0).astype(_BF16)
    sorted_row = lax.broadcasted_iota(_I32, (SUB_SORTED, SUB), 0).astype(_F32)
    fullest = jnp.zeros((1, 1), _F32)
    for s in range(N_SUB):
        rows = slice(s * SUB, (s + 1) * SUB)
        hit1 = expert_row == i1[:, rows]
        hit2 = expert_row == i2[:, rows]
        onehot = jnp.where(jnp.logical_or(hit1, hit2), 1.0, 0.0)
        fullest = jnp.maximum(fullest, jnp.max(jnp.sum(onehot, axis=1, keepdims=True), axis=0, keepdims=True))
        rank = _dot(onehot.astype(_BF16), earlier)
        q1 = (i1[:, rows] - N_GROUPS) * SLOTS + jnp.sum(jnp.where(hit1, rank, 0.0), axis=0, keepdims=True)
        q2 = (i2[:, rows] - N_GROUPS) * SLOTS + jnp.sum(jnp.where(hit2, rank, 0.0), axis=0, keepdims=True)
        at1 = sorted_row == q1
        at2 = sorted_row == q2
        perm = jnp.where(jnp.logical_or(at1, at2), 1.0, 0.0).astype(_BF16)
        perm_ref[s] = perm
        xs_ref[s] = _dot(perm, xb[rows]).astype(_BF16)
        weight = jnp.sum(jnp.where(at1, w1[:, rows], 0.0) + jnp.where(at2, w2[:, rows], 0.0),
                         axis=-1, keepdims=True)
        ws_ref[s] = jnp.broadcast_to(weight, (SUB_SORTED, V7X_LANES))

    flag_v[...] = jnp.broadcast_to(fullest, flag_v.shape).astype(_I32)
    to_smem = pltpu.make_async_copy(flag_v, flag_s, sem)
    to_smem.start()
    to_smem.wait()
    fits = flag_s[0, 0] <= SLOTS

    def finish(rows, moe):
        x2 = _layer_norm(DEEPNORM_ALPHA * x1_ref[rows, :] + moe, g2_ref[...], b2_ref[...])
        gate = jax.nn.sigmoid(_dot(x2.astype(_BF16), wg_ref[...]) + bg_ref[...])
        p = jnp.where(is_sample, ps_ref[0, rows, :], pp_ref[0, rows, :])
        out = x2 + gate * _dot(p.astype(_BF16), wp_ref[...])
        if split_out:
            @pl.when(jnp.logical_not(is_sample))
            def _():
                out_refs[0][rows, :] = out

            @pl.when(is_sample)
            def _():
                out_refs[1][rows, :] = out
        else:
            out_refs[0][rows, :] = out

    @pl.when(fits)
    def _():
        for e in range(N_EXPERTS):
            slots = slice(e * SLOTS, (e + 1) * SLOTS)
            xt = jnp.concatenate([xs_ref[s, slots, :] for s in range(N_SUB)], axis=0)
            wt = jnp.concatenate([ws_ref[s, slots, 0:1] for s in range(N_SUB)], axis=0)
            y = _dot(_expert_act(_dot(xt, wgu_ref[e]), wt), wd_ref[e]).astype(_BF16)
            for s in range(N_SUB):
                ys_ref[s, slots, :] = y[s * SLOTS:(s + 1) * SLOTS, :]
        for s in range(N_SUB):
            moe = lax.dot_general(perm_ref[s], ys_ref[s], (((0,), (0,)), ((), ())), preferred_element_type=_F32)
            finish(slice(s * SUB, (s + 1) * SUB), moe)

    @pl.when(jnp.logical_not(fits))
    def _():
        row_all = lax.broadcasted_iota(_I32, (V7X_LANES, MOE_TM), 0).astype(_F32)
        combine = (jnp.where(row_all == i1, w1, 0.0) + jnp.where(row_all == i2, w2, 0.0)).T
        acc = jnp.zeros((MOE_TM, D_MODEL), _F32)
        for e in range(N_EXPERTS):
            w = combine[:, N_GROUPS + e:N_GROUPS + e + 1]
            acc = acc + _dot(_expert_act(_dot(xb, wgu_ref[e]), w), wd_ref[e])
        finish(slice(0, MOE_TM), acc)


def _moe_ple(layer, split_out, x1, wr, br, wgu, wd, g2, b2, wg, bg, wp, p_prompt, p_sample):
    prompt_block = lambda i: jnp.minimum(i, N_PROMPT_MOE_BLOCKS - 1)
    sample_block = lambda i: jnp.maximum(i - N_PROMPT_MOE_BLOCKS, 0)
    if split_out:
        out_specs = [pl.BlockSpec((MOE_TM, D_MODEL), lambda i: (prompt_block(i), 0)),
                     pl.BlockSpec((MOE_TM, D_MODEL), lambda i: (sample_block(i), 0))]
        out_shape = [jax.ShapeDtypeStruct((N_PROMPT_TOK, D_MODEL), _F32),
                     jax.ShapeDtypeStruct((N_SAMPLE_TOK, D_MODEL), _F32)]
    else:
        out_specs = [pl.BlockSpec((MOE_TM, D_MODEL), lambda i: (i, 0))]
        out_shape = [jax.ShapeDtypeStruct((N_TOK, D_MODEL), _F32)]
    return pl.pallas_call(
        functools.partial(_moe_ple_kernel, split_out),
        grid=(N_TOK // MOE_TM,),
        in_specs=[
            pl.BlockSpec((MOE_TM, D_MODEL), lambda i: (i, 0)),
            _const_spec((ROUTER_ROWS, D_MODEL)),
            _const_spec((ROUTER_ROWS, 1)),
            _const_spec((N_EXPERTS, D_MODEL, 2 * D_EXPERT)),
            _const_spec((N_EXPERTS, D_EXPERT, D_MODEL)),
            _const_spec((1, D_MODEL)),
            _const_spec((1, D_MODEL)),
            _const_spec((D_MODEL, D_MODEL)),
            _const_spec((1, D_MODEL)),
            _const_spec((PLE_DIM, D_MODEL)),
            pl.BlockSpec((1, MOE_TM, PLE_DIM), lambda i: (layer, prompt_block(i), 0)),
            pl.BlockSpec((1, MOE_TM, PLE_DIM), lambda i: (layer, sample_block(i), 0)),
        ],
        out_specs=out_specs,
        out_shape=out_shape,
        scratch_shapes=[
            pltpu.VMEM((N_SUB, SUB_SORTED, D_MODEL), _BF16),
            pltpu.VMEM((N_SUB, SUB_SORTED, V7X_LANES), _F32),
            pltpu.VMEM((N_SUB, SUB_SORTED, SUB), _BF16),
            pltpu.VMEM((V7X_SUBLANES, V7X_LANES), _I32),
            pltpu.SMEM((V7X_SUBLANES, V7X_LANES), _I32),
            pltpu.SemaphoreType.DMA,
        ],
        compiler_params=_PARAMS,
        name="moe_ple",
    )(x1, wr, br, wgu, wd, g2, b2, wg, bg, wp, p_prompt, p_sample)


def _row(v):
    return v.reshape(1, -1)


def _pad_lanes(a, lanes):
    return jnp.pad(a, [(0, 0)] * (a.ndim - 1) + [(0, lanes - a.shape[-1])])


def _pad_rows(a):
    return jnp.pad(a, [(0, V7X_SUBLANES - a.shape[0]), (0, 0)])


def kernel(x_prompt, x_sample, state_conv, p_prompt, p_sample, a_w_in, a_b_in, a_ln_g, a_ln_b, a_w_s, a_b_s, a_w_out, b_w_in, b_conv_w, b_w_out, ln1_g, ln1_b, ln2_g, ln2_b, moe_w_group, moe_b_group, moe_w_expert, moe_b_expert, moe_w_gate_up, moe_w_down, ple_w_gate, ple_b_gate, ple_w_proj):
    xs = (x_prompt.reshape(N_PROMPT_TOK, D_MODEL), x_sample.reshape(N_SAMPLE_TOK, D_MODEL))
    pp = p_prompt.reshape(DEPTH, N_PROMPT_TOK, PLE_DIM)
    ps = p_sample.reshape(DEPTH, N_SAMPLE_TOK, PLE_DIM)

    reps = CHUNK // DEC_SEQ
    ws_sample = jnp.tile(a_w_s[:, :, :DEC_SEQ, :DEC_SEQ], (1, 1, reps, reps))
    ws2 = jnp.stack([a_w_s, ws_sample], axis=1)
    bs_sample = jnp.tile(a_b_s[:, :, :DEC_SEQ], (1, 1, reps))
    bs2 = _pad_lanes(jnp.swapaxes(jnp.stack([a_b_s, bs_sample], axis=1), -1, -2), V7X_LANES)

    zeros = jnp.zeros((state_conv.shape[0], DEC_BATCH, DEC_SEQ, B_WIDTH), _F32)
    past1 = zeros.at[:, :, 0].set(state_conv[:, :, 1]).reshape(-1, N_SAMPLE_TOK, B_WIDTH)
    past2 = (zeros.at[:, :, 0].set(state_conv[:, :, 0]).at[:, :, 1].set(state_conv[:, :, 1])
             .reshape(-1, N_SAMPLE_TOK, B_WIDTH))

    w_router = jnp.swapaxes(_pad_lanes(jnp.concatenate([moe_w_group, moe_w_expert], axis=-1), ROUTER_ROWS),
                            -1, -2).astype(_BF16)
    b_router = _pad_lanes(jnp.concatenate([moe_b_group, moe_b_expert], axis=-1), ROUTER_ROWS)[..., None]

    v_prompt, v_sample, st_prompt, st_sample = [], [], [], []
    for i in range(DEPTH):
        j = i // 2
        if i % 2 == 0:
            x1, vp, vs = _mixer_a(xs, a_w_in[j].astype(_BF16), _row(a_b_in[j]), _row(a_ln_g[j]),
                                  _row(a_ln_b[j]), ws2[j], bs2[j], a_w_out[j].astype(_BF16),
                                  _row(ln1_g[i]), _row(ln1_b[i]))
            v_prompt.append(vp)
            v_sample.append(vs)
        else:
            x1, stp, sts = _mixer_b(xs[0], b_w_in[j].astype(_BF16), _pad_rows(b_conv_w[j]),
                                    b_w_out[j].astype(_BF16), _row(ln1_g[i]), _row(ln1_b[i]), past1[j], past2[j])
            st_prompt.append(stp)
            st_sample.append(sts)
        xs = _moe_ple(i, i == DEPTH - 1, x1, w_router[i], b_router[i], moe_w_gate_up[i].astype(_BF16),
                      moe_w_down[i].astype(_BF16), _row(ln2_g[i]), _row(ln2_b[i]), ple_w_gate[i].astype(_BF16), _row(ple_b_gate[i]),
                      ple_w_proj[i].astype(_BF16), pp, ps)

    keep = CONV_WIDTH - 1
    return (xs[0].reshape(BATCH, SEQ, D_MODEL),
            xs[1].reshape(DEC_BATCH, DEC_SEQ, D_MODEL),
            jnp.stack(v_prompt).reshape(-1, BATCH, CHUNK, A_WIDTH),
            jnp.stack(v_sample).reshape(-1, DEC_BATCH, DEC_SEQ, A_WIDTH),
            jnp.stack(st_prompt).reshape(-1, BATCH, V7X_SUBLANES, B_WIDTH)[:, :, V7X_SUBLANES - keep:],
            jnp.stack(st_sample).reshape(-1, DEC_BATCH, DEC_SEQ, B_WIDTH)[:, :, DEC_SEQ - keep:])
```

```python
import functools
import math

import jax
import jax.numpy as jnp
from jax import lax
from jax.experimental import pallas as pl
from jax.experimental.pallas import tpu as pltpu

D_MODEL = 1024
BATCH = 8
SEQ = 2048
DEPTH = 4
DEC_BATCH = 128
DEC_SEQ = 8
CHUNK = 128
A_HEADS = 4
A_WIDTH = 3 * D_MODEL
A_HEAD_DIM = A_WIDTH // A_HEADS
CONV_WIDTH = 3
B_WIDTH = D_MODEL
N_GROUPS = 4
EXPERTS_PER_GROUP = 4
N_EXPERTS = N_GROUPS * EXPERTS_PER_GROUP
D_EXPERT = D_MODEL // 4
PLE_DIM = 256
LN_EPS = 1e-5
DEEPNORM_ALPHA = (2 * DEPTH) ** 0.25

V7X_LANES = 128
V7X_SUBLANES = 8
V7X_VMEM_BYTES = 64 * 1024 * 1024

N_PROMPT_TOK = BATCH * SEQ
N_SAMPLE_TOK = DEC_BATCH * DEC_SEQ
N_TOK = N_PROMPT_TOK + N_SAMPLE_TOK
TM = 256
N_PROMPT_BLOCKS = N_PROMPT_TOK // TM
N_SAMPLE_BLOCKS = N_SAMPLE_TOK // TM
N_BLOCKS = N_PROMPT_BLOCKS + N_SAMPLE_BLOCKS
BLOCKS_PER_SEQ = SEQ // TM
ROUTER_ROWS = 32
VMEM_LIMIT = V7X_VMEM_BYTES - 4 * 1024 * 1024

MOE_TM = 512
SUB = 256
N_SUB = MOE_TM // SUB
SLOTS = 64
SUB_SORTED = N_EXPERTS * SLOTS
N_PROMPT_MOE_BLOCKS = N_PROMPT_TOK // MOE_TM

assert N_PROMPT_TOK % MOE_TM == 0 and N_SAMPLE_TOK % MOE_TM == 0 and MOE_TM % SUB == 0
assert SEQ % TM == 0 and N_SAMPLE_TOK % TM == 0 and TM % CHUNK == 0 and TM % DEC_SEQ == 0
assert CHUNK % DEC_SEQ == 0 and CONV_WIDTH - 1 <= V7X_SUBLANES

_BF16 = jnp.bfloat16
_F32 = jnp.float32
_I32 = jnp.int32


def _layer_norm(x, g, b):
    mu = jnp.mean(x, axis=-1, keepdims=True)
    xc = x - mu
    var = jnp.mean(xc * xc, axis=-1, keepdims=True)
    return xc * lax.rsqrt(var + LN_EPS) * g + b


def _gelu(x):
    return 0.5 * x * (1.0 + lax.erf(x * math.sqrt(0.5)))


def _dot(a, b):
    return jnp.dot(a, b, preferred_element_type=_F32)


def _const_spec(shape):
    zeros = (0,) * len(shape)
    return pl.BlockSpec(shape, lambda i: zeros, pipeline_mode=pl.Buffered(1))


def _layer_spec(layer, shape):
    zeros = (0,) * len(shape)
    return pl.BlockSpec((None,) + tuple(shape), lambda i: (layer,) + zeros, pipeline_mode=pl.Buffered(1))


def _tok_spec(width):
    return pl.BlockSpec((TM, width), lambda i: (i, 0))


def _prompt_block(i):
    return jnp.minimum(i, N_PROMPT_BLOCKS - 1)


def _sample_block(i):
    return jnp.maximum(i - N_PROMPT_BLOCKS, 0)


def _prompt_spec(width):
    return pl.BlockSpec((TM, width), lambda i: (_prompt_block(i), 0))


def _sample_spec(width):
    return pl.BlockSpec((TM, width), lambda i: (_sample_block(i), 0))


def _prompt_state_spec(rows, width):
    return pl.BlockSpec((rows, width), lambda i: (_prompt_block(i) // BLOCKS_PER_SEQ, 0))


_PARAMS = pltpu.CompilerParams(dimension_semantics=("arbitrary",), vmem_limit_bytes=VMEM_LIMIT)


def _is_sample():
    return pl.program_id(0) >= N_PROMPT_BLOCKS


def _read_tokens(split, refs):
    if not split:
        return refs[0][...]
    return jnp.where(_is_sample(), refs[1][...], refs[0][...])


def _write_by_side(prompt_ref, sample_ref, prompt_val, sample_val):
    @pl.when(jnp.logical_not(_is_sample()))
    def _():
        prompt_ref[...] = prompt_val

    @pl.when(_is_sample())
    def _():
        sample_ref[...] = sample_val


def _mixer_a_kernel(split_in, *refs):
    n_x = 2 if split_in else 1
    x_refs = refs[:n_x]
    (w_in_ref, b_in_ref, lng_ref, lnb_ref, ws_ref, bs_ref, w_out_ref, g1_ref, b1_ref,
     x1_ref, vp_ref, vs_ref, t_ref) = refs[n_x:]
    r = lax.broadcasted_iota(_I32, (CHUNK, CHUNK), 0)
    c = lax.broadcasted_iota(_I32, (CHUNK, CHUNK), 1)
    shift = jnp.where(_is_sample(), int(math.log2(DEC_SEQ)), int(math.log2(CHUNK)))
    mask = jnp.logical_and(r >= c, (r >> shift) == (c >> shift))
    w_spatial = [jnp.where(mask, ws_ref[0, h], 0.0).astype(_BF16) for h in range(A_HEADS)]

    for k in range(TM // CHUNK):
        rows = slice(k * CHUNK, (k + 1) * CHUNK)
        if split_in:
            x = jnp.where(_is_sample(), x_refs[1][rows, :], x_refs[0][rows, :])
        else:
            x = x_refs[0][rows, :]
        xb = x.astype(_BF16)
        u = _gelu(_dot(xb, w_in_ref[:, :A_WIDTH]) + b_in_ref[:, :A_WIDTH])
        v = _gelu(_dot(xb, w_in_ref[:, A_WIDTH:]) + b_in_ref[:, A_WIDTH:])
        v = _layer_norm(v, lng_ref[...], lnb_ref[...])
        vs_ref[rows, :] = v
        vb = v.astype(_BF16)
        for h in range(A_HEADS):
            cols = slice(h * A_HEAD_DIM, (h + 1) * A_HEAD_DIM)
            sg = _dot(w_spatial[h], vb[:, cols]) + bs_ref[0, :, h:h + 1]
            t_ref[rows, cols] = (u[:, cols] * sg).astype(_BF16)
        y = _dot(t_ref[rows, :], w_out_ref[...])
        x1_ref[rows, :] = _layer_norm(DEEPNORM_ALPHA * x + y, g1_ref[...], b1_ref[...])

    @pl.when(jnp.logical_not(_is_sample()))
    def _():
        vp_ref[...] = vs_ref[TM - CHUNK:, :]


def _mixer_a(j, xs, w_in, b_in, ln_g, ln_b, ws2, bs2, w_out, g1, b1):
    split_in = len(xs) == 2
    x_specs = [_prompt_spec(D_MODEL), _sample_spec(D_MODEL)] if split_in else [_tok_spec(D_MODEL)]
    side = lambda i: jnp.where(i < N_PROMPT_BLOCKS, 0, 1)
    return pl.pallas_call(
        functools.partial(_mixer_a_kernel, split_in),
        grid=(N_BLOCKS,),
        in_specs=x_specs + [
            _layer_spec(j, (D_MODEL, 2 * A_WIDTH)),
            _const_spec((1, 2 * A_WIDTH)),
            _const_spec((1, A_WIDTH)),
            _const_spec((1, A_WIDTH)),
            pl.BlockSpec((1, A_HEADS, CHUNK, CHUNK), lambda i: (side(i), 0, 0, 0)),
            pl.BlockSpec((1, CHUNK, V7X_LANES), lambda i: (side(i), 0, 0)),
            _layer_spec(j, (A_WIDTH, D_MODEL)),
            _const_spec((1, D_MODEL)),
            _const_spec((1, D_MODEL)),
        ],
        out_specs=[
            _tok_spec(D_MODEL),
            _prompt_state_spec(CHUNK, A_WIDTH),
            _sample_spec(A_WIDTH),
        ],
        out_shape=[
            jax.ShapeDtypeStruct((N_TOK, D_MODEL), _F32),
            jax.ShapeDtypeStruct((BATCH * CHUNK, A_WIDTH), _F32),
            jax.ShapeDtypeStruct((N_SAMPLE_TOK, A_WIDTH), _F32),
        ],
        scratch_shapes=[pltpu.VMEM((TM, A_WIDTH), _BF16)],
        compiler_params=_PARAMS,
        name="mixer_a",
    )(*xs, w_in, b_in, ln_g, ln_b, ws2, bs2, w_out, g1, b1)


def _mixer_b_kernel(x_ref, w_in_ref, cw_ref, w_out_ref, g1_ref, b1_ref, past1_ref, past2_ref,
                    x1_ref, stp_ref, sts_ref, ext_ref):
    i = pl.program_id(0)
    is_sample = _is_sample()
    @pl.when(jnp.logical_or(is_sample, i % BLOCKS_PER_SEQ == 0))
    def _():
        ext_ref[0:V7X_SUBLANES, :] = jnp.zeros((V7X_SUBLANES, B_WIDTH), _F32)

    pos = lax.broadcasted_iota(_I32, (TM, B_WIDTH), 0) % DEC_SEQ
    pos = jnp.where(is_sample, pos, DEC_SEQ)
    x = x_ref[...]
    h = _dot(x.astype(_BF16), w_in_ref[...])
    gate_b = h[:, :B_WIDTH]
    p = h[:, B_WIDTH:2 * B_WIDTH] * h[:, 2 * B_WIDTH:]
    ext_ref[V7X_SUBLANES:, :] = p
    prev1 = jnp.where(pos < 1, past1_ref[...], ext_ref[V7X_SUBLANES - 1:V7X_SUBLANES - 1 + TM, :])
    prev2 = jnp.where(pos < 2, past2_ref[...], ext_ref[V7X_SUBLANES - 2:V7X_SUBLANES - 2 + TM, :])
    conv = cw_ref[0:1, :] * prev2 + cw_ref[1:2, :] * prev1 + cw_ref[2:3, :] * p
    y = _dot((gate_b * conv).astype(_BF16), w_out_ref[...])
    x1_ref[...] = _layer_norm(DEEPNORM_ALPHA * x + y, g1_ref[...], b1_ref[...])
    ext_ref[0:V7X_SUBLANES, :] = ext_ref[TM:TM + V7X_SUBLANES, :]

    @pl.when(jnp.logical_not(is_sample))
    def _():
        stp_ref[...] = ext_ref[TM:TM + V7X_SUBLANES, :]

    @pl.when(is_sample)
    def _():
        sts_ref[...] = ext_ref[V7X_SUBLANES:, :]


def _mixer_b(j, x, w_in, conv_w, w_out, g1, b1, past1, past2):
    return pl.pallas_call(
        _mixer_b_kernel,
        grid=(N_BLOCKS,),
        in_specs=[
            _tok_spec(D_MODEL),
            _layer_spec(j, (D_MODEL, 3 * B_WIDTH)),
            _const_spec((V7X_SUBLANES, B_WIDTH)),
            _layer_spec(j, (B_WIDTH, D_MODEL)),
            _const_spec((1, D_MODEL)),
            _const_spec((1, D_MODEL)),
            _sample_spec(B_WIDTH),
            _sample_spec(B_WIDTH),
        ],
        out_specs=[
            _tok_spec(D_MODEL),
            _prompt_state_spec(V7X_SUBLANES, B_WIDTH),
            _sample_spec(B_WIDTH),
        ],
        out_shape=[
            jax.ShapeDtypeStruct((N_TOK, D_MODEL), _F32),
            jax.ShapeDtypeStruct((BATCH * V7X_SUBLANES, B_WIDTH), _F32),
            jax.ShapeDtypeStruct((N_SAMPLE_TOK, B_WIDTH), _F32),
        ],
        scratch_shapes=[pltpu.VMEM((V7X_SUBLANES + TM, B_WIDTH), _F32)],
        compiler_params=_PARAMS,
        name="mixer_b",
    )(x, w_in, conv_w, w_out, g1, b1, past1, past2)


def _route(logits):
    row = lax.broadcasted_iota(_I32, logits.shape, 0)
    row_f = row.astype(_F32)
    neg = jnp.float32(-jnp.inf)
    far = jnp.float32(ROUTER_ROWS)

    def first_argmax(vals):
        top = jnp.max(vals, axis=0, keepdims=True)
        idx = jnp.min(jnp.where(vals == top, row_f, far), axis=0, keepdims=True)
        return top, idx

    is_group = row < N_GROUPS
    g_top, g_idx = first_argmax(jnp.where(is_group, logits, neg))
    g_w = 1.0 / jnp.sum(jnp.where(is_group, jnp.exp(logits - g_top), 0.0), axis=0, keepdims=True)
    lo = N_GROUPS + EXPERTS_PER_GROUP * g_idx
    in_group = jnp.logical_and(row_f >= lo, row_f < lo + EXPERTS_PER_GROUP)
    e_logits = jnp.where(in_group, logits, neg)
    v1, i1 = first_argmax(e_logits)
    v2, i2 = first_argmax(jnp.where(row_f == i1, neg, e_logits))
    e2 = jnp.exp(v2 - v1)
    return i1, i2, g_w / (1.0 + e2), g_w * e2 / (1.0 + e2)


def _expert_act(hgu, scale):
    gate = hgu[:, :D_EXPERT]
    return (gate * jax.nn.sigmoid(gate) * hgu[:, D_EXPERT:] * scale).astype(_BF16)


def _moe_ple_kernel(split_out, x1_ref, wr_ref, br_ref, wgu_ref, wd_ref, g2_ref, b2_ref, wg_ref, bg_ref,
                    wp_ref, pp_ref, ps_ref, *rest):
    n_out = 2 if split_out else 1
    out_refs = rest[:n_out]
    xs_ref, ws_ref, perm_ref, flag_v, flag_s, sem = rest[n_out:]
    ys_ref = xs_ref
    is_sample = pl.program_id(0) >= N_PROMPT_MOE_BLOCKS
    xb = x1_ref[...].astype(_BF16)
    logits = lax.dot_general(wr_ref[...], xb, (((1,), (1,)), ((), ())), preferred_element_type=_F32)
    i1, i2, w1, w2 = _route(logits + br_ref[...])

    expert_row = lax.broadcasted_iota(_I32, (N_EXPERTS, SUB), 0).astype(_F32) + N_GROUPS
    r = lax.broadcasted_iota(_I32, (SUB, SUB), 0)
    c = lax.broadcasted_iota(_I32, (SUB, SUB), 1)
    earlier = jnp.where(r < c, 1.0, 0.0).astype(_BF16)
    sorted_row = lax.broadcasted_iota(_I32, (SUB_SORTED, SUB), 0).astype(_F32)
    fullest = jnp.zeros((1, 1), _F32)
    for s in range(N_SUB):
        rows = slice(s * SUB, (s + 1) * SUB)
        hit1 = expert_row == i1[:, rows]
        hit2 = expert_row == i2[:, rows]
        onehot = jnp.where(jnp.logical_or(hit1, hit2), 1.0, 0.0)
        fullest = jnp.maximum(fullest, jnp.max(jnp.sum(onehot, axis=1, keepdims=True), axis=0, keepdims=True))
        rank = _dot(onehot.astype(_BF16), earlier)
        q1 = (i1[:, rows] - N_GROUPS) * SLOTS + jnp.sum(jnp.where(hit1, rank, 0.0), axis=0, keepdims=True)
        q2 = (i2[:, rows] - N_GROUPS) * SLOTS + jnp.sum(jnp.where(hit2, rank, 0.0), axis=0, keepdims=True)
        at1 = sorted_row == q1
        at2 = sorted_row == q2
        perm = jnp.where(jnp.logical_or(at1, at2), 1.0, 0.0).astype(_BF16)
        perm_ref[s] = perm
        xs_ref[s] = _dot(perm, xb[rows]).astype(_BF16)
        weight = jnp.sum(jnp.where(at1, w1[:, rows], 0.0) + jnp.where(at2, w2[:, rows], 0.0),
                         axis=-1, keepdims=True)
        ws_ref[s] = jnp.broadcast_to(weight, (SUB_SORTED, V7X_LANES))

    flag_v[...] = jnp.broadcast_to(fullest, flag_v.shape).astype(_I32)
    to_smem = pltpu.make_async_copy(flag_v, flag_s, sem)
    to_smem.start()
    to_smem.wait()
    fits = flag_s[0, 0] <= SLOTS

    def finish(rows, moe):
        x2 = _layer_norm(DEEPNORM_ALPHA * x1_ref[rows, :] + moe, g2_ref[...], b2_ref[...])
        gate = jax.nn.sigmoid(_dot(x2.astype(_BF16), wg_ref[...]) + bg_ref[...])
        p = jnp.where(is_sample, ps_ref[0, rows, :], pp_ref[0, rows, :])
        out = x2 + gate * _dot(p.astype(_BF16), wp_ref[...])
        if split_out:
            @pl.when(jnp.logical_not(is_sample))
            def _():
                out_refs[0][rows, :] = out

            @pl.when(is_sample)
            def _():
                out_refs[1][rows, :] = out
        else:
            out_refs[0][rows, :] = out

    @pl.when(fits)
    def _():
        for e in range(N_EXPERTS):
            slots = slice(e * SLOTS, (e + 1) * SLOTS)
            xt = jnp.concatenate([xs_ref[s, slots, :] for s in range(N_SUB)], axis=0)
            wt = jnp.concatenate([ws_ref[s, slots, 0:1] for s in range(N_SUB)], axis=0)
            y = _dot(_expert_act(_dot(xt, wgu_ref[e]), wt), wd_ref[e]).astype(_BF16)
            for s in range(N_SUB):
                ys_ref[s, slots, :] = y[s * SLOTS:(s + 1) * SLOTS, :]
        for s in range(N_SUB):
            moe = lax.dot_general(perm_ref[s], ys_ref[s], (((0,), (0,)), ((), ())), preferred_element_type=_F32)
            finish(slice(s * SUB, (s + 1) * SUB), moe)

    @pl.when(jnp.logical_not(fits))
    def _():
        row_all = lax.broadcasted_iota(_I32, (V7X_LANES, MOE_TM), 0).astype(_F32)
        combine = (jnp.where(row_all == i1, w1, 0.0) + jnp.where(row_all == i2, w2, 0.0)).T
        acc = jnp.zeros((MOE_TM, D_MODEL), _F32)
        for e in range(N_EXPERTS):
            w = combine[:, N_GROUPS + e:N_GROUPS + e + 1]
            acc = acc + _dot(_expert_act(_dot(xb, wgu_ref[e]), w), wd_ref[e])
        finish(slice(0, MOE_TM), acc)


def _moe_ple(layer, split_out, x1, wr, br, wgu, wd, g2, b2, wg, bg, wp, p_prompt, p_sample):
    prompt_block = lambda i: jnp.minimum(i, N_PROMPT_MOE_BLOCKS - 1)
    sample_block = lambda i: jnp.maximum(i - N_PROMPT_MOE_BLOCKS, 0)
    if split_out:
        out_specs = [pl.BlockSpec((MOE_TM, D_MODEL), lambda i: (prompt_block(i), 0)),
                     pl.BlockSpec((MOE_TM, D_MODEL), lambda i: (sample_block(i), 0))]
        out_shape = [jax.ShapeDtypeStruct((N_PROMPT_TOK, D_MODEL), _F32),
                     jax.ShapeDtypeStruct((N_SAMPLE_TOK, D_MODEL), _F32)]
    else:
        out_specs = [pl.BlockSpec((MOE_TM, D_MODEL), lambda i: (i, 0))]
        out_shape = [jax.ShapeDtypeStruct((N_TOK, D_MODEL), _F32)]
    return pl.pallas_call(
        functools.partial(_moe_ple_kernel, split_out),
        grid=(N_TOK // MOE_TM,),
        in_specs=[
            pl.BlockSpec((MOE_TM, D_MODEL), lambda i: (i, 0)),
            _layer_spec(layer, (ROUTER_ROWS, D_MODEL)),
            _layer_spec(layer, (ROUTER_ROWS, 1)),
            _layer_spec(layer, (N_EXPERTS, D_MODEL, 2 * D_EXPERT)),
            _layer_spec(layer, (N_EXPERTS, D_EXPERT, D_MODEL)),
            _const_spec((1, D_MODEL)),
            _const_spec((1, D_MODEL)),
            _layer_spec(layer, (D_MODEL, D_MODEL)),
            _const_spec((1, D_MODEL)),
            _layer_spec(layer, (PLE_DIM, D_MODEL)),
            pl.BlockSpec((1, MOE_TM, PLE_DIM), lambda i: (layer, prompt_block(i), 0)),
            pl.BlockSpec((1, MOE_TM, PLE_DIM), lambda i: (layer, sample_block(i), 0)),
        ],
        out_specs=out_specs,
        out_shape=out_shape,
        scratch_shapes=[
            pltpu.VMEM((N_SUB, SUB_SORTED, D_MODEL), _BF16),
            pltpu.VMEM((N_SUB, SUB_SORTED, V7X_LANES), _F32),
            pltpu.VMEM((N_SUB, SUB_SORTED, SUB), _BF16),
            pltpu.VMEM((V7X_SUBLANES, V7X_LANES), _I32),
            pltpu.SMEM((V7X_SUBLANES, V7X_LANES), _I32),
            pltpu.SemaphoreType.DMA,
        ],
        compiler_params=_PARAMS,
        name="moe_ple",
    )(x1, wr, br, wgu, wd, g2, b2, wg, bg, wp, p_prompt, p_sample)


def _row(v):
    return v.reshape(1, -1)


def _pad_lanes(a, lanes):
    return jnp.pad(a, [(0, 0)] * (a.ndim - 1) + [(0, lanes - a.shape[-1])])


def _pad_rows(a):
    return jnp.pad(a, [(0, V7X_SUBLANES - a.shape[0]), (0, 0)])


def kernel(x_prompt, x_sample, state_conv, p_prompt, p_sample, a_w_in, a_b_in, a_ln_g, a_ln_b, a_w_s, a_b_s, a_w_out, b_w_in, b_conv_w, b_w_out, ln1_g, ln1_b, ln2_g, ln2_b, moe_w_group, moe_b_group, moe_w_expert, moe_b_expert, moe_w_gate_up, moe_w_down, ple_w_gate, ple_b_gate, ple_w_proj):
    xs = (x_prompt.reshape(N_PROMPT_TOK, D_MODEL), x_sample.reshape(N_SAMPLE_TOK, D_MODEL))
    pp = p_prompt.reshape(DEPTH, N_PROMPT_TOK, PLE_DIM)
    ps = p_sample.reshape(DEPTH, N_SAMPLE_TOK, PLE_DIM)

    reps = CHUNK // DEC_SEQ
    ws_sample = jnp.tile(a_w_s[:, :, :DEC_SEQ, :DEC_SEQ], (1, 1, reps, reps))
    ws2 = jnp.stack([a_w_s, ws_sample], axis=1)
    bs_sample = jnp.tile(a_b_s[:, :, :DEC_SEQ], (1, 1, reps))
    bs2 = _pad_lanes(jnp.swapaxes(jnp.stack([a_b_s, bs_sample], axis=1), -1, -2), V7X_LANES)

    zeros = jnp.zeros((state_conv.shape[0], DEC_BATCH, DEC_SEQ, B_WIDTH), _F32)
    past1 = zeros.at[:, :, 0].set(state_conv[:, :, 1]).reshape(-1, N_SAMPLE_TOK, B_WIDTH)
    past2 = (zeros.at[:, :, 0].set(state_conv[:, :, 0]).at[:, :, 1].set(state_conv[:, :, 1])
             .reshape(-1, N_SAMPLE_TOK, B_WIDTH))

    w_router = jnp.swapaxes(_pad_lanes(jnp.concatenate([moe_w_group, moe_w_expert], axis=-1), ROUTER_ROWS),
                            -1, -2).astype(_BF16)
    b_router = _pad_lanes(jnp.concatenate([moe_b_group, moe_b_expert], axis=-1), ROUTER_ROWS)[..., None]

    a_w_in, a_w_out, b_w_in, b_w_out, moe_w_gate_up, moe_w_down, ple_w_gate, ple_w_proj = (
        w.astype(_BF16) for w in (a_w_in, a_w_out, b_w_in, b_w_out, moe_w_gate_up, moe_w_down,
                                  ple_w_gate, ple_w_proj))

    v_prompt, v_sample, st_prompt, st_sample = [], [], [], []
    for i in range(DEPTH):
        j = i // 2
        if i % 2 == 0:
            x1, vp, vs = _mixer_a(j, xs, a_w_in, _row(a_b_in[j]), _row(a_ln_g[j]), _row(a_ln_b[j]),
                                  ws2[j], bs2[j], a_w_out, _row(ln1_g[i]), _row(ln1_b[i]))
            v_prompt.append(vp)
            v_sample.append(vs)
        else:
            x1, stp, sts = _mixer_b(j, xs[0], b_w_in, _pad_rows(b_conv_w[j]), b_w_out,
                                    _row(ln1_g[i]), _row(ln1_b[i]), past1[j], past2[j])
            st_prompt.append(stp)
            st_sample.append(sts)
        xs = _moe_ple(i, i == DEPTH - 1, x1, w_router, b_router, moe_w_gate_up, moe_w_down,
                      _row(ln2_g[i]), _row(ln2_b[i]), ple_w_gate, _row(ple_b_gate[i]), ple_w_proj, pp, ps)

    keep = CONV_WIDTH - 1
    return (xs[0].reshape(BATCH, SEQ, D_MODEL),
            xs[1].reshape(DEC_BATCH, DEC_SEQ, D_MODEL),
            jnp.stack(v_prompt).reshape(-1, BATCH, CHUNK, A_WIDTH),
            jnp.stack(v_sample).reshape(-1, DEC_BATCH, DEC_SEQ, A_WIDTH),
            jnp.stack(st_prompt).reshape(-1, BATCH, V7X_SUBLANES, B_WIDTH)[:, :, V7X_SUBLANES - keep:],
            jnp.stack(st_sample).reshape(-1, DEC_BATCH, DEC_SEQ, B_WIDTH)[:, :, DEC_SEQ - keep:])
```

```python
import functools
import math

import jax
import jax.numpy as jnp
from jax import lax
from jax.experimental import pallas as pl
from jax.experimental.pallas import tpu as pltpu

D_MODEL = 1024
BATCH = 8
SEQ = 2048
DEPTH = 4
DEC_BATCH = 128
DEC_SEQ = 8
CHUNK = 128
A_HEADS = 4
A_WIDTH = 3 * D_MODEL
A_HEAD_DIM = A_WIDTH // A_HEADS
CONV_WIDTH = 3
B_WIDTH = D_MODEL
N_GROUPS = 4
EXPERTS_PER_GROUP = 4
N_EXPERTS = N_GROUPS * EXPERTS_PER_GROUP
D_EXPERT = D_MODEL // 4
PLE_DIM = 256
LN_EPS = 1e-5
DEEPNORM_ALPHA = (2 * DEPTH) ** 0.25

V7X_LANES = 128
V7X_SUBLANES = 8
V7X_VMEM_BYTES = 64 * 1024 * 1024

N_PROMPT_TOK = BATCH * SEQ
N_SAMPLE_TOK = DEC_BATCH * DEC_SEQ
N_TOK = N_PROMPT_TOK + N_SAMPLE_TOK
TM = 512
CHAIN_ROWS = 256
N_PROMPT_BLOCKS = N_PROMPT_TOK // TM
N_SAMPLE_BLOCKS = N_SAMPLE_TOK // TM
N_BLOCKS = N_PROMPT_BLOCKS + N_SAMPLE_BLOCKS
BLOCKS_PER_SEQ = SEQ // TM
ROUTER_ROWS = 32
VMEM_LIMIT = V7X_VMEM_BYTES - 4 * 1024 * 1024

MOE_TM = 512
SUB = 256
N_SUB = MOE_TM // SUB
SLOTS = 64
SUB_SORTED = N_EXPERTS * SLOTS
N_PROMPT_MOE_BLOCKS = N_PROMPT_TOK // MOE_TM

assert N_PROMPT_TOK % MOE_TM == 0 and N_SAMPLE_TOK % MOE_TM == 0 and MOE_TM % SUB == 0
assert SEQ % TM == 0 and N_SAMPLE_TOK % TM == 0 and TM % CHUNK == 0 and TM % DEC_SEQ == 0
assert CHUNK % DEC_SEQ == 0 and CONV_WIDTH - 1 <= V7X_SUBLANES

_BF16 = jnp.bfloat16
_F32 = jnp.float32
_I32 = jnp.int32


def _layer_norm(x, g, b):
    mu = jnp.mean(x, axis=-1, keepdims=True)
    xc = x - mu
    var = jnp.mean(xc * xc, axis=-1, keepdims=True)
    return xc * lax.rsqrt(var + LN_EPS) * g + b


def _gelu(x):
    h = 0.5 * x
    return h + h * lax.erf(x * math.sqrt(0.5))


def _dot(a, b):
    return jnp.dot(a, b, preferred_element_type=_F32)


def _const_spec(shape):
    zeros = (0,) * len(shape)
    return pl.BlockSpec(shape, lambda i: zeros, pipeline_mode=pl.Buffered(1))


def _layer_spec(layer, shape):
    zeros = (0,) * len(shape)
    return pl.BlockSpec((None,) + tuple(shape), lambda i: (layer,) + zeros, pipeline_mode=pl.Buffered(1))


def _tok_spec(width):
    return pl.BlockSpec((TM, width), lambda i: (i, 0))


def _prompt_block(i):
    return jnp.minimum(i, N_PROMPT_BLOCKS - 1)


def _sample_block(i):
    return jnp.maximum(i - N_PROMPT_BLOCKS, 0)


def _prompt_spec(width):
    return pl.BlockSpec((TM, width), lambda i: (_prompt_block(i), 0))


def _sample_spec(width):
    return pl.BlockSpec((TM, width), lambda i: (_sample_block(i), 0))


def _prompt_state_spec(rows, width):
    return pl.BlockSpec((rows, width), lambda i: (_prompt_block(i) // BLOCKS_PER_SEQ, 0))


_PARAMS = pltpu.CompilerParams(dimension_semantics=("arbitrary",), vmem_limit_bytes=VMEM_LIMIT)


def _is_sample():
    return pl.program_id(0) >= N_PROMPT_BLOCKS


def _read_tokens(split, refs):
    if not split:
        return refs[0][...]
    return jnp.where(_is_sample(), refs[1][...], refs[0][...])


def _write_by_side(prompt_ref, sample_ref, prompt_val, sample_val):
    @pl.when(jnp.logical_not(_is_sample()))
    def _():
        prompt_ref[...] = prompt_val

    @pl.when(_is_sample())
    def _():
        sample_ref[...] = sample_val


def _mixer_a_kernel(split_in, *refs):
    n_x = 2 if split_in else 1
    x_refs = refs[:n_x]
    (w_in_ref, b_in_ref, lng_ref, lnb_ref, ws_ref, bs_ref, w_out_ref, g1_ref, b1_ref,
     x1_ref, vp_ref, vs_ref, t_ref) = refs[n_x:]
    r = lax.broadcasted_iota(_I32, (CHUNK, CHUNK), 0)
    c = lax.broadcasted_iota(_I32, (CHUNK, CHUNK), 1)
    shift = jnp.where(_is_sample(), int(math.log2(DEC_SEQ)), int(math.log2(CHUNK)))
    mask = jnp.logical_and(r >= c, (r >> shift) == (c >> shift))
    w_spatial = [jnp.where(mask, ws_ref[0, h], 0.0).astype(_BF16) for h in range(A_HEADS)]

    for k in range(TM // CHAIN_ROWS):
        rows = slice(k * CHAIN_ROWS, (k + 1) * CHAIN_ROWS)
        if split_in:
            x = jnp.where(_is_sample(), x_refs[1][rows, :], x_refs[0][rows, :])
        else:
            x = x_refs[0][rows, :]
        xb = x.astype(_BF16)
        u = _gelu(_dot(xb, w_in_ref[:, :A_WIDTH]) + b_in_ref[:, :A_WIDTH])
        v = _gelu(_dot(xb, w_in_ref[:, A_WIDTH:]) + b_in_ref[:, A_WIDTH:])
        v = _layer_norm(v, lng_ref[...], lnb_ref[...])
        vs_ref[rows, :] = v
        vb = v.astype(_BF16)
        for h in range(A_HEADS):
            cols = slice(h * A_HEAD_DIM, (h + 1) * A_HEAD_DIM)
            for q in range(CHAIN_ROWS // CHUNK):
                chunk = slice(q * CHUNK, (q + 1) * CHUNK)
                sg = _dot(w_spatial[h], vb[chunk, cols]) + bs_ref[0, :, h:h + 1]
                t_ref[k * CHAIN_ROWS + q * CHUNK:k * CHAIN_ROWS + (q + 1) * CHUNK, cols] = (
                    u[chunk, cols] * sg).astype(_BF16)
        y = _dot(t_ref[rows, :], w_out_ref[...])
        x1_ref[rows, :] = _layer_norm(DEEPNORM_ALPHA * x + y, g1_ref[...], b1_ref[...])

    @pl.when(jnp.logical_not(_is_sample()))
    def _():
        vp_ref[...] = vs_ref[TM - CHUNK:, :]


def _mixer_a(j, xs, w_in, b_in, ln_g, ln_b, ws2, bs2, w_out, g1, b1):
    split_in = len(xs) == 2
    x_specs = [_prompt_spec(D_MODEL), _sample_spec(D_MODEL)] if split_in else [_tok_spec(D_MODEL)]
    side = lambda i: jnp.where(i < N_PROMPT_BLOCKS, 0, 1)
    return pl.pallas_call(
        functools.partial(_mixer_a_kernel, split_in),
        grid=(N_BLOCKS,),
        in_specs=x_specs + [
            _layer_spec(j, (D_MODEL, 2 * A_WIDTH)),
            _const_spec((1, 2 * A_WIDTH)),
            _const_spec((1, A_WIDTH)),
            _const_spec((1, A_WIDTH)),
            pl.BlockSpec((1, A_HEADS, CHUNK, CHUNK), lambda i: (side(i), 0, 0, 0)),
            pl.BlockSpec((1, CHUNK, V7X_LANES), lambda i: (side(i), 0, 0)),
            _layer_spec(j, (A_WIDTH, D_MODEL)),
            _const_spec((1, D_MODEL)),
            _const_spec((1, D_MODEL)),
        ],
        out_specs=[
            _tok_spec(D_MODEL),
            _prompt_state_spec(CHUNK, A_WIDTH),
            _sample_spec(A_WIDTH),
        ],
        out_shape=[
            jax.ShapeDtypeStruct((N_TOK, D_MODEL), _F32),
            jax.ShapeDtypeStruct((BATCH * CHUNK, A_WIDTH), _F32),
            jax.ShapeDtypeStruct((N_SAMPLE_TOK, A_WIDTH), _F32),
        ],
        scratch_shapes=[pltpu.VMEM((TM, A_WIDTH), _BF16)],
        compiler_params=_PARAMS,
        name="mixer_a",
    )(*xs, w_in, b_in, ln_g, ln_b, ws2, bs2, w_out, g1, b1)


def _mixer_b_kernel(x_ref, w_in_ref, cw_ref, w_out_ref, g1_ref, b1_ref, past1_ref, past2_ref,
                    x1_ref, stp_ref, sts_ref, ext_ref):
    i = pl.program_id(0)
    is_sample = _is_sample()
    @pl.when(jnp.logical_or(is_sample, i % BLOCKS_PER_SEQ == 0))
    def _():
        ext_ref[0:V7X_SUBLANES, :] = jnp.zeros((V7X_SUBLANES, B_WIDTH), _F32)

    pos = lax.broadcasted_iota(_I32, (TM, B_WIDTH), 0) % DEC_SEQ
    pos = jnp.where(is_sample, pos, DEC_SEQ)
    x = x_ref[...]
    h = _dot(x.astype(_BF16), w_in_ref[...])
    gate_b = h[:, :B_WIDTH]
    p = h[:, B_WIDTH:2 * B_WIDTH] * h[:, 2 * B_WIDTH:]
    ext_ref[V7X_SUBLANES:, :] = p
    prev1 = jnp.where(pos < 1, past1_ref[...], ext_ref[V7X_SUBLANES - 1:V7X_SUBLANES - 1 + TM, :])
    prev2 = jnp.where(pos < 2, past2_ref[...], ext_ref[V7X_SUBLANES - 2:V7X_SUBLANES - 2 + TM, :])
    conv = cw_ref[0:1, :] * prev2 + cw_ref[1:2, :] * prev1 + cw_ref[2:3, :] * p
    y = _dot((gate_b * conv).astype(_BF16), w_out_ref[...])
    x1_ref[...] = _layer_norm(DEEPNORM_ALPHA * x + y, g1_ref[...], b1_ref[...])
    ext_ref[0:V7X_SUBLANES, :] = ext_ref[TM:TM + V7X_SUBLANES, :]

    @pl.when(jnp.logical_not(is_sample))
    def _():
        stp_ref[...] = ext_ref[TM:TM + V7X_SUBLANES, :]

    @pl.when(is_sample)
    def _():
        sts_ref[...] = ext_ref[V7X_SUBLANES:, :]


def _mixer_b(j, x, w_in, conv_w, w_out, g1, b1, past1, past2):
    return pl.pallas_call(
        _mixer_b_kernel,
        grid=(N_BLOCKS,),
        in_specs=[
            _tok_spec(D_MODEL),
            _layer_spec(j, (D_MODEL, 3 * B_WIDTH)),
            _const_spec((V7X_SUBLANES, B_WIDTH)),
            _layer_spec(j, (B_WIDTH, D_MODEL)),
            _const_spec((1, D_MODEL)),
            _const_spec((1, D_MODEL)),
            _sample_spec(B_WIDTH),
            _sample_spec(B_WIDTH),
        ],
        out_specs=[
            _tok_spec(D_MODEL),
            _prompt_state_spec(V7X_SUBLANES, B_WIDTH),
            _sample_spec(B_WIDTH),
        ],
        out_shape=[
            jax.ShapeDtypeStruct((N_TOK, D_MODEL), _F32),
            jax.ShapeDtypeStruct((BATCH * V7X_SUBLANES, B_WIDTH), _F32),
            jax.ShapeDtypeStruct((N_SAMPLE_TOK, B_WIDTH), _F32),
        ],
        scratch_shapes=[pltpu.VMEM((V7X_SUBLANES + TM, B_WIDTH), _F32)],
        compiler_params=_PARAMS,
        name="mixer_b",
    )(x, w_in, conv_w, w_out, g1, b1, past1, past2)


def _route(logits):
    row = lax.broadcasted_iota(_I32, logits.shape, 0)
    row_f = row.astype(_F32)
    neg = jnp.float32(-jnp.inf)
    far = jnp.float32(ROUTER_ROWS)

    def first_argmax(vals):
        top = jnp.max(vals, axis=0, keepdims=True)
        idx = jnp.min(jnp.where(vals == top, row_f, far), axis=0, keepdims=True)
        return top, idx

    is_group = row < N_GROUPS
    g_top, g_idx = first_argmax(jnp.where(is_group, logits, neg))
    g_w = 1.0 / jnp.sum(jnp.where(is_group, jnp.exp(logits - g_top), 0.0), axis=0, keepdims=True)
    lo = N_GROUPS + EXPERTS_PER_GROUP * g_idx
    in_group = jnp.logical_and(row_f >= lo, row_f < lo + EXPERTS_PER_GROUP)
    e_logits = jnp.where(in_group, logits, neg)
    v1, i1 = first_argmax(e_logits)
    v2, i2 = first_argmax(jnp.where(row_f == i1, neg, e_logits))
    e2 = jnp.exp(v2 - v1)
    return i1, i2, g_w / (1.0 + e2), g_w * e2 / (1.0 + e2)


def _expert_act(hgu, scale):
    gate = hgu[:, :D_EXPERT]
    return (gate * jax.nn.sigmoid(gate) * hgu[:, D_EXPERT:] * scale).astype(_BF16)


def _moe_ple_kernel(split_out, x1_ref, wr_ref, br_ref, wgu_ref, wd_ref, g2_ref, b2_ref, wg_ref, bg_ref,
                    wp_ref, pp_ref, ps_ref, *rest):
    n_out = 2 if split_out else 1
    out_refs = rest[:n_out]
    xs_ref, ws_ref, perm_ref, flag_v, flag_s, sem = rest[n_out:]
    ys_ref = xs_ref
    is_sample = pl.program_id(0) >= N_PROMPT_MOE_BLOCKS
    xb = x1_ref[...].astype(_BF16)
    logits = lax.dot_general(wr_ref[...], xb, (((1,), (1,)), ((), ())), preferred_element_type=_F32)
    i1, i2, w1, w2 = _route(logits + br_ref[...])

    expert_row = lax.broadcasted_iota(_I32, (N_EXPERTS, SUB), 0).astype(_F32) + N_GROUPS
    r = lax.broadcasted_iota(_I32, (SUB, SUB), 0)
    c = lax.broadcasted_iota(_I32, (SUB, SUB), 1)
    earlier = jnp.where(r < c, 1.0, 0.0).astype(_BF16)
    sorted_row = lax.broadcasted_iota(_I32, (SUB_SORTED, SUB), 0).astype(_F32)
    hits = []
    fullest = jnp.zeros((1, 1), _F32)
    for s in range(N_SUB):
        rows = slice(s * SUB, (s + 1) * SUB)
        hit1 = expert_row == i1[:, rows]
        hit2 = expert_row == i2[:, rows]
        onehot = jnp.where(jnp.logical_or(hit1, hit2), 1.0, 0.0)
        fullest = jnp.maximum(fullest, jnp.max(jnp.sum(onehot, axis=1, keepdims=True), axis=0, keepdims=True))
        hits.append((hit1, hit2, onehot))
    flag_v[...] = jnp.broadcast_to(fullest, flag_v.shape).astype(_I32)
    to_smem = pltpu.make_async_copy(flag_v, flag_s, sem)
    to_smem.start()

    for s in range(N_SUB):
        rows = slice(s * SUB, (s + 1) * SUB)
        hit1, hit2, onehot = hits[s]
        rank = _dot(onehot.astype(_BF16), earlier)
        q1 = (i1[:, rows] - N_GROUPS) * SLOTS + jnp.sum(jnp.where(hit1, rank, 0.0), axis=0, keepdims=True)
        q2 = (i2[:, rows] - N_GROUPS) * SLOTS + jnp.sum(jnp.where(hit2, rank, 0.0), axis=0, keepdims=True)
        at1 = sorted_row == q1
        at2 = sorted_row == q2
        perm = jnp.where(jnp.logical_or(at1, at2), 1.0, 0.0).astype(_BF16)
        perm_ref[s] = perm
        xs_ref[s] = _dot(perm, xb[rows]).astype(_BF16)
        weight = jnp.sum(jnp.where(at1, w1[:, rows], 0.0) + jnp.where(at2, w2[:, rows], 0.0),
                         axis=-1, keepdims=True)
        ws_ref[s] = jnp.broadcast_to(weight, (SUB_SORTED, V7X_LANES))

    to_smem.wait()
    fits = flag_s[0, 0] <= SLOTS

    def finish(rows, moe):
        x2 = _layer_norm(DEEPNORM_ALPHA * x1_ref[rows, :] + moe, g2_ref[...], b2_ref[...])
        gate = jax.nn.sigmoid(_dot(x2.astype(_BF16), wg_ref[...]) + bg_ref[...])
        p = jnp.where(is_sample, ps_ref[0, rows, :], pp_ref[0, rows, :])
        out = x2 + gate * _dot(p.astype(_BF16), wp_ref[...])
        if split_out:
            @pl.when(jnp.logical_not(is_sample))
            def _():
                out_refs[0][rows, :] = out

            @pl.when(is_sample)
            def _():
                out_refs[1][rows, :] = out
        else:
            out_refs[0][rows, :] = out

    @pl.when(fits)
    def _():
        for e in range(N_EXPERTS):
            slots = slice(e * SLOTS, (e + 1) * SLOTS)
            xt = jnp.concatenate([xs_ref[s, slots, :] for s in range(N_SUB)], axis=0)
            wt = jnp.concatenate([ws_ref[s, slots, 0:1] for s in range(N_SUB)], axis=0)
            y = _dot(_expert_act(_dot(xt, wgu_ref[e]), wt), wd_ref[e]).astype(_BF16)
            for s in range(N_SUB):
                ys_ref[s, slots, :] = y[s * SLOTS:(s + 1) * SLOTS, :]
        for s in range(N_SUB):
            moe = lax.dot_general(perm_ref[s], ys_ref[s], (((0,), (0,)), ((), ())), preferred_element_type=_F32)
            finish(slice(s * SUB, (s + 1) * SUB), moe)

    @pl.when(jnp.logical_not(fits))
    def _():
        row_all = lax.broadcasted_iota(_I32, (V7X_LANES, MOE_TM), 0).astype(_F32)
        combine = (jnp.where(row_all == i1, w1, 0.0) + jnp.where(row_all == i2, w2, 0.0)).T
        acc = jnp.zeros((MOE_TM, D_MODEL), _F32)
        for e in range(N_EXPERTS):
            w = combine[:, N_GROUPS + e:N_GROUPS + e + 1]
            acc = acc + _dot(_expert_act(_dot(xb, wgu_ref[e]), w), wd_ref[e])
        finish(slice(0, MOE_TM), acc)


def _moe_ple(layer, split_out, x1, wr, br, wgu, wd, g2, b2, wg, bg, wp, p_prompt, p_sample):
    prompt_block = lambda i: jnp.minimum(i, N_PROMPT_MOE_BLOCKS - 1)
    sample_block = lambda i: jnp.maximum(i - N_PROMPT_MOE_BLOCKS, 0)
    if split_out:
        out_specs = [pl.BlockSpec((MOE_TM, D_MODEL), lambda i: (prompt_block(i), 0)),
                     pl.BlockSpec((MOE_TM, D_MODEL), lambda i: (sample_block(i), 0))]
        out_shape = [jax.ShapeDtypeStruct((N_PROMPT_TOK, D_MODEL), _F32),
                     jax.ShapeDtypeStruct((N_SAMPLE_TOK, D_MODEL), _F32)]
    else:
        out_specs = [pl.BlockSpec((MOE_TM, D_MODEL), lambda i: (i, 0))]
        out_shape = [jax.ShapeDtypeStruct((N_TOK, D_MODEL), _F32)]
    return pl.pallas_call(
        functools.partial(_moe_ple_kernel, split_out),
        grid=(N_TOK // MOE_TM,),
        in_specs=[
            pl.BlockSpec((MOE_TM, D_MODEL), lambda i: (i, 0)),
            _layer_spec(layer, (ROUTER_ROWS, D_MODEL)),
            _layer_spec(layer, (ROUTER_ROWS, 1)),
            _layer_spec(layer, (N_EXPERTS, D_MODEL, 2 * D_EXPERT)),
            _layer_spec(layer, (N_EXPERTS, D_EXPERT, D_MODEL)),
            _const_spec((1, D_MODEL)),
            _const_spec((1, D_MODEL)),
            _layer_spec(layer, (D_MODEL, D_MODEL)),
            _const_spec((1, D_MODEL)),
            _layer_spec(layer, (PLE_DIM, D_MODEL)),
            pl.BlockSpec((1, MOE_TM, PLE_DIM), lambda i: (layer, prompt_block(i), 0)),
            pl.BlockSpec((1, MOE_TM, PLE_DIM), lambda i: (layer, sample_block(i), 0)),
        ],
        out_specs=out_specs,
        out_shape=out_shape,
        scratch_shapes=[
            pltpu.VMEM((N_SUB, SUB_SORTED, D_MODEL), _BF16),
            pltpu.VMEM((N_SUB, SUB_SORTED, V7X_LANES), _F32),
            pltpu.VMEM((N_SUB, SUB_SORTED, SUB), _BF16),
            pltpu.VMEM((V7X_SUBLANES, V7X_LANES), _I32),
            pltpu.SMEM((V7X_SUBLANES, V7X_LANES), _I32),
            pltpu.SemaphoreType.DMA,
        ],
        compiler_params=_PARAMS,
        name="moe_ple",
    )(x1, wr, br, wgu, wd, g2, b2, wg, bg, wp, p_prompt, p_sample)


def _row(v):
    return v.reshape(1, -1)


def _pad_lanes(a, lanes):
    return jnp.pad(a, [(0, 0)] * (a.ndim - 1) + [(0, lanes - a.shape[-1])])


def _pad_rows(a):
    return jnp.pad(a, [(0, V7X_SUBLANES - a.shape[0]), (0, 0)])


def kernel(x_prompt, x_sample, state_conv, p_prompt, p_sample, a_w_in, a_b_in, a_ln_g, a_ln_b, a_w_s, a_b_s, a_w_out, b_w_in, b_conv_w, b_w_out, ln1_g, ln1_b, ln2_g, ln2_b, moe_w_group, moe_b_group, moe_w_expert, moe_b_expert, moe_w_gate_up, moe_w_down, ple_w_gate, ple_b_gate, ple_w_proj):
    xs = (x_prompt.reshape(N_PROMPT_TOK, D_MODEL), x_sample.reshape(N_SAMPLE_TOK, D_MODEL))
    pp = p_prompt.reshape(DEPTH, N_PROMPT_TOK, PLE_DIM)
    ps = p_sample.reshape(DEPTH, N_SAMPLE_TOK, PLE_DIM)

    reps = CHUNK // DEC_SEQ
    ws_sample = jnp.tile(a_w_s[:, :, :DEC_SEQ, :DEC_SEQ], (1, 1, reps, reps))
    ws2 = jnp.stack([a_w_s, ws_sample], axis=1)
    bs_sample = jnp.tile(a_b_s[:, :, :DEC_SEQ], (1, 1, reps))
    bs2 = _pad_lanes(jnp.swapaxes(jnp.stack([a_b_s, bs_sample], axis=1), -1, -2), V7X_LANES)

    zeros = jnp.zeros((state_conv.shape[0], DEC_BATCH, DEC_SEQ, B_WIDTH), _F32)
    past1 = zeros.at[:, :, 0].set(state_conv[:, :, 1]).reshape(-1, N_SAMPLE_TOK, B_WIDTH)
    past2 = (zeros.at[:, :, 0].set(state_conv[:, :, 0]).at[:, :, 1].set(state_conv[:, :, 1])
             .reshape(-1, N_SAMPLE_TOK, B_WIDTH))

    w_router = jnp.swapaxes(_pad_lanes(jnp.concatenate([moe_w_group, moe_w_expert], axis=-1), ROUTER_ROWS),
                            -1, -2).astype(_BF16)
    b_router = _pad_lanes(jnp.concatenate([moe_b_group, moe_b_expert], axis=-1), ROUTER_ROWS)[..., None]

    a_w_in, a_w_out, b_w_in, b_w_out, moe_w_gate_up, moe_w_down, ple_w_gate, ple_w_proj = (
        w.astype(_BF16) for w in (a_w_in, a_w_out, b_w_in, b_w_out, moe_w_gate_up, moe_w_down,
                                  ple_w_gate, ple_w_proj))

    v_prompt, v_sample, st_prompt, st_sample = [], [], [], []
    for i in range(DEPTH):
        j = i // 2
        if i % 2 == 0:
            x1, vp, vs = _mixer_a(j, xs, a_w_in, _row(a_b_in[j]), _row(a_ln_g[j]), _row(a_ln_b[j]),
                                  ws2[j], bs2[j], a_w_out, _row(ln1_g[i]), _row(ln1_b[i]))
            v_prompt.append(vp)
            v_sample.append(vs)
        else:
            x1, stp, sts = _mixer_b(j, xs[0], b_w_in, _pad_rows(b_conv_w[j]), b_w_out,
                                    _row(ln1_g[i]), _row(ln1_b[i]), past1[j], past2[j])
            st_prompt.append(stp)
            st_sample.append(sts)
        xs = _moe_ple(i, i == DEPTH - 1, x1, w_router, b_router, moe_w_gate_up, moe_w_down,
                      _row(ln2_g[i]), _row(ln2_b[i]), ple_w_gate, _row(ple_b_gate[i]), ple_w_proj, pp, ps)

    keep = CONV_WIDTH - 1
    return (xs[0].reshape(BATCH, SEQ, D_MODEL),
            xs[1].reshape(DEC_BATCH, DEC_SEQ, D_MODEL),
            jnp.stack(v_prompt).reshape(-1, BATCH, CHUNK, A_WIDTH),
            jnp.stack(v_sample).reshape(-1, DEC_BATCH, DEC_SEQ, A_WIDTH),
            jnp.stack(st_prompt).reshape(-1, BATCH, V7X_SUBLANES, B_WIDTH)[:, :, V7X_SUBLANES - keep:],
            jnp.stack(st_sample).reshape(-1, DEC_BATCH, DEC_SEQ, B_WIDTH)[:, :, DEC_SEQ - keep:])
```

```python
import functools
import math

import jax
import jax.numpy as jnp
from jax import lax
from jax.experimental import pallas as pl
from jax.experimental.pallas import tpu as pltpu

D_MODEL = 1024
BATCH = 8
SEQ = 2048
DEPTH = 4
DEC_BATCH = 128
DEC_SEQ = 8
CHUNK = 128
A_HEADS = 4
A_WIDTH = 3 * D_MODEL
A_HEAD_DIM = A_WIDTH // A_HEADS
CONV_WIDTH = 3
B_WIDTH = D_MODEL
N_GROUPS = 4
EXPERTS_PER_GROUP = 4
N_EXPERTS = N_GROUPS * EXPERTS_PER_GROUP
D_EXPERT = D_MODEL // 4
PLE_DIM = 256
LN_EPS = 1e-5
DEEPNORM_ALPHA = (2 * DEPTH) ** 0.25

V7X_LANES = 128
V7X_SUBLANES = 8
V7X_VMEM_BYTES = 64 * 1024 * 1024

N_PROMPT_TOK = BATCH * SEQ
N_SAMPLE_TOK = DEC_BATCH * DEC_SEQ
N_TOK = N_PROMPT_TOK + N_SAMPLE_TOK
TM = 512
CHAIN_ROWS = 256
N_PROMPT_BLOCKS = N_PROMPT_TOK // TM
N_SAMPLE_BLOCKS = N_SAMPLE_TOK // TM
N_BLOCKS = N_PROMPT_BLOCKS + N_SAMPLE_BLOCKS
BLOCKS_PER_SEQ = SEQ // TM
ROUTER_ROWS = 32
VMEM_LIMIT = V7X_VMEM_BYTES - 4 * 1024 * 1024

MOE_TM = 512
SUB = 256
N_SUB = MOE_TM // SUB
SLOTS = 64
SUB_SORTED = N_EXPERTS * SLOTS
N_PROMPT_MOE_BLOCKS = N_PROMPT_TOK // MOE_TM

assert N_PROMPT_TOK % MOE_TM == 0 and N_SAMPLE_TOK % MOE_TM == 0 and MOE_TM % SUB == 0
assert SEQ % TM == 0 and N_SAMPLE_TOK % TM == 0 and TM % CHUNK == 0 and TM % DEC_SEQ == 0
assert CHUNK % DEC_SEQ == 0 and CONV_WIDTH - 1 <= V7X_SUBLANES

_BF16 = jnp.bfloat16
_F32 = jnp.float32
_I32 = jnp.int32


def _layer_norm(x, g, b):
    mu = jnp.mean(x, axis=-1, keepdims=True)
    xc = x - mu
    var = jnp.mean(xc * xc, axis=-1, keepdims=True)
    return xc * lax.rsqrt(var + LN_EPS) * g + b


def _gelu(x):
    h = 0.5 * x
    return h + h * lax.erf(x * math.sqrt(0.5))


def _dot(a, b):
    return jnp.dot(a, b, preferred_element_type=_F32)


def _const_spec(shape):
    zeros = (0,) * len(shape)
    return pl.BlockSpec(shape, lambda i: zeros, pipeline_mode=pl.Buffered(1))


def _layer_spec(layer, shape):
    zeros = (0,) * len(shape)
    return pl.BlockSpec((None,) + tuple(shape), lambda i: (layer,) + zeros, pipeline_mode=pl.Buffered(1))


def _tok_spec(width):
    return pl.BlockSpec((TM, width), lambda i: (i, 0))


def _prompt_block(i):
    return jnp.minimum(i, N_PROMPT_BLOCKS - 1)


def _sample_block(i):
    return jnp.maximum(i - N_PROMPT_BLOCKS, 0)


def _prompt_spec(width):
    return pl.BlockSpec((TM, width), lambda i: (_prompt_block(i), 0))


def _sample_spec(width):
    return pl.BlockSpec((TM, width), lambda i: (_sample_block(i), 0))


def _prompt_state_spec(rows, width):
    return pl.BlockSpec((rows, width), lambda i: (_prompt_block(i) // BLOCKS_PER_SEQ, 0))


_PARAMS = pltpu.CompilerParams(dimension_semantics=("arbitrary",), vmem_limit_bytes=VMEM_LIMIT)


def _is_sample():
    return pl.program_id(0) >= N_PROMPT_BLOCKS


def _read_tokens(split, refs):
    if not split:
        return refs[0][...]
    return jnp.where(_is_sample(), refs[1][...], refs[0][...])


def _write_by_side(prompt_ref, sample_ref, prompt_val, sample_val):
    @pl.when(jnp.logical_not(_is_sample()))
    def _():
        prompt_ref[...] = prompt_val

    @pl.when(_is_sample())
    def _():
        sample_ref[...] = sample_val


def _mixer_a_kernel(split_in, *refs):
    n_x = 2 if split_in else 1
    x_refs = refs[:n_x]
    (w_in_ref, b_in_ref, lng_ref, lnb_ref, ws_ref, bs_ref, w_out_ref, g1_ref, b1_ref,
     x1_ref, vp_ref, vs_ref, t_ref) = refs[n_x:]
    r = lax.broadcasted_iota(_I32, (CHUNK, CHUNK), 0)
    c = lax.broadcasted_iota(_I32, (CHUNK, CHUNK), 1)
    shift = jnp.where(_is_sample(), int(math.log2(DEC_SEQ)), int(math.log2(CHUNK)))
    mask = jnp.logical_and(r >= c, (r >> shift) == (c >> shift))
    w_spatial = [jnp.where(mask, ws_ref[0, h], 0.0).astype(_BF16) for h in range(A_HEADS)]

    n_chains = TM // CHAIN_ROWS
    rows = [slice(k * CHAIN_ROWS, (k + 1) * CHAIN_ROWS) for k in range(n_chains)]
    st = [dict() for _ in range(n_chains)]

    def proj_v(k):
        if split_in:
            x = jnp.where(_is_sample(), x_refs[1][rows[k], :], x_refs[0][rows[k], :])
        else:
            x = x_refs[0][rows[k], :]
        st[k]["x"] = x
        st[k]["xb"] = x.astype(_BF16)
        st[k]["zv"] = _dot(st[k]["xb"], w_in_ref[:, A_WIDTH:]) + b_in_ref[:, A_WIDTH:]

    def proj_u(k):
        st[k]["zu"] = _dot(st[k]["xb"], w_in_ref[:, :A_WIDTH]) + b_in_ref[:, :A_WIDTH]

    def act_v(k):
        v = _layer_norm(_gelu(st[k].pop("zv")), lng_ref[...], lnb_ref[...])
        vs_ref[rows[k], :] = v
        st[k]["vb"] = v.astype(_BF16)

    def act_u(k):
        st[k]["u"] = _gelu(st[k].pop("zu"))

    def gate(k):
        u, vb = st[k].pop("u"), st[k].pop("vb")
        for h in range(A_HEADS):
            cols = slice(h * A_HEAD_DIM, (h + 1) * A_HEAD_DIM)
            for q in range(CHAIN_ROWS // CHUNK):
                chunk = slice(q * CHUNK, (q + 1) * CHUNK)
                sg = _dot(w_spatial[h], vb[chunk, cols]) + bs_ref[0, :, h:h + 1]
                t_ref[k * CHAIN_ROWS + q * CHUNK:k * CHAIN_ROWS + (q + 1) * CHUNK, cols] = (
                    u[chunk, cols] * sg).astype(_BF16)

    def out(k):
        y = _dot(t_ref[rows[k], :], w_out_ref[...])
        x1_ref[rows[k], :] = _layer_norm(DEEPNORM_ALPHA * st[k].pop("x") + y, g1_ref[...], b1_ref[...])

    for k in range(n_chains):
        for stage in (proj_v, act_v, proj_u, act_u, gate, out):
            stage(k)

    @pl.when(jnp.logical_not(_is_sample()))
    def _():
        vp_ref[...] = vs_ref[TM - CHUNK:, :]


def _mixer_a(j, xs, w_in, b_in, ln_g, ln_b, ws2, bs2, w_out, g1, b1):
    split_in = len(xs) == 2
    x_specs = [_prompt_spec(D_MODEL), _sample_spec(D_MODEL)] if split_in else [_tok_spec(D_MODEL)]
    side = lambda i: jnp.where(i < N_PROMPT_BLOCKS, 0, 1)
    return pl.pallas_call(
        functools.partial(_mixer_a_kernel, split_in),
        grid=(N_BLOCKS,),
        in_specs=x_specs + [
            _layer_spec(j, (D_MODEL, 2 * A_WIDTH)),
            _const_spec((1, 2 * A_WIDTH)),
            _const_spec((1, A_WIDTH)),
            _const_spec((1, A_WIDTH)),
            pl.BlockSpec((1, A_HEADS, CHUNK, CHUNK), lambda i: (side(i), 0, 0, 0)),
            pl.BlockSpec((1, CHUNK, V7X_LANES), lambda i: (side(i), 0, 0)),
            _layer_spec(j, (A_WIDTH, D_MODEL)),
            _const_spec((1, D_MODEL)),
            _const_spec((1, D_MODEL)),
        ],
        out_specs=[
            _tok_spec(D_MODEL),
            _prompt_state_spec(CHUNK, A_WIDTH),
            _sample_spec(A_WIDTH),
        ],
        out_shape=[
            jax.ShapeDtypeStruct((N_TOK, D_MODEL), _F32),
            jax.ShapeDtypeStruct((BATCH * CHUNK, A_WIDTH), _F32),
            jax.ShapeDtypeStruct((N_SAMPLE_TOK, A_WIDTH), _F32),
        ],
        scratch_shapes=[pltpu.VMEM((TM, A_WIDTH), _BF16)],
        compiler_params=_PARAMS,
        name="mixer_a",
    )(*xs, w_in, b_in, ln_g, ln_b, ws2, bs2, w_out, g1, b1)


def _mixer_b_kernel(x_ref, w_in_ref, cw_ref, w_out_ref, g1_ref, b1_ref, past1_ref, past2_ref,
                    x1_ref, stp_ref, sts_ref, ext_ref):
    i = pl.program_id(0)
    is_sample = _is_sample()
    @pl.when(jnp.logical_or(is_sample, i % BLOCKS_PER_SEQ == 0))
    def _():
        ext_ref[0:V7X_SUBLANES, :] = jnp.zeros((V7X_SUBLANES, B_WIDTH), _F32)

    pos = lax.broadcasted_iota(_I32, (TM, B_WIDTH), 0) % DEC_SEQ
    pos = jnp.where(is_sample, pos, DEC_SEQ)
    x = x_ref[...]
    h = _dot(x.astype(_BF16), w_in_ref[...])
    gate_b = h[:, :B_WIDTH]
    p = h[:, B_WIDTH:2 * B_WIDTH] * h[:, 2 * B_WIDTH:]
    ext_ref[V7X_SUBLANES:, :] = p
    prev1 = jnp.where(pos < 1, past1_ref[...], ext_ref[V7X_SUBLANES - 1:V7X_SUBLANES - 1 + TM, :])
    prev2 = jnp.where(pos < 2, past2_ref[...], ext_ref[V7X_SUBLANES - 2:V7X_SUBLANES - 2 + TM, :])
    conv = cw_ref[0:1, :] * prev2 + cw_ref[1:2, :] * prev1 + cw_ref[2:3, :] * p
    y = _dot((gate_b * conv).astype(_BF16), w_out_ref[...])
    x1_ref[...] = _layer_norm(DEEPNORM_ALPHA * x + y, g1_ref[...], b1_ref[...])
    ext_ref[0:V7X_SUBLANES, :] = ext_ref[TM:TM + V7X_SUBLANES, :]

    @pl.when(jnp.logical_not(is_sample))
    def _():
        stp_ref[...] = ext_ref[TM:TM + V7X_SUBLANES, :]

    @pl.when(is_sample)
    def _():
        sts_ref[...] = ext_ref[V7X_SUBLANES:, :]


def _mixer_b(j, x, w_in, conv_w, w_out, g1, b1, past1, past2):
    return pl.pallas_call(
        _mixer_b_kernel,
        grid=(N_BLOCKS,),
        in_specs=[
            _tok_spec(D_MODEL),
            _layer_spec(j, (D_MODEL, 3 * B_WIDTH)),
            _const_spec((V7X_SUBLANES, B_WIDTH)),
            _layer_spec(j, (B_WIDTH, D_MODEL)),
            _const_spec((1, D_MODEL)),
            _const_spec((1, D_MODEL)),
            _sample_spec(B_WIDTH),
            _sample_spec(B_WIDTH),
        ],
        out_specs=[
            _tok_spec(D_MODEL),
            _prompt_state_spec(V7X_SUBLANES, B_WIDTH),
            _sample_spec(B_WIDTH),
        ],
        out_shape=[
            jax.ShapeDtypeStruct((N_TOK, D_MODEL), _F32),
            jax.ShapeDtypeStruct((BATCH * V7X_SUBLANES, B_WIDTH), _F32),
            jax.ShapeDtypeStruct((N_SAMPLE_TOK, B_WIDTH), _F32),
        ],
        scratch_shapes=[pltpu.VMEM((V7X_SUBLANES + TM, B_WIDTH), _F32)],
        compiler_params=_PARAMS,
        name="mixer_b",
    )(x, w_in, conv_w, w_out, g1, b1, past1, past2)


def _route(logits):
    row = lax.broadcasted_iota(_I32, logits.shape, 0)
    row_f = row.astype(_F32)
    neg = jnp.float32(-jnp.inf)
    far = jnp.float32(ROUTER_ROWS)

    def first_argmax(vals):
        top = jnp.max(vals, axis=0, keepdims=True)
        idx = jnp.min(jnp.where(vals == top, row_f, far), axis=0, keepdims=True)
        return top, idx

    is_group = row < N_GROUPS
    g_top, g_idx = first_argmax(jnp.where(is_group, logits, neg))
    g_w = 1.0 / jnp.sum(jnp.where(is_group, jnp.exp(logits - g_top), 0.0), axis=0, keepdims=True)
    lo = N_GROUPS + EXPERTS_PER_GROUP * g_idx
    in_group = jnp.logical_and(row_f >= lo, row_f < lo + EXPERTS_PER_GROUP)
    e_logits = jnp.where(in_group, logits, neg)
    v1, i1 = first_argmax(e_logits)
    v2, i2 = first_argmax(jnp.where(row_f == i1, neg, e_logits))
    e2 = jnp.exp(v2 - v1)
    return i1, i2, g_w / (1.0 + e2), g_w * e2 / (1.0 + e2)


def _expert_act(hgu, scale):
    gate = hgu[:, :D_EXPERT]
    return (gate * jax.nn.sigmoid(gate) * hgu[:, D_EXPERT:] * scale).astype(_BF16)


def _moe_ple_kernel(split_out, x1_ref, wr_ref, br_ref, wgu_ref, wd_ref, g2_ref, b2_ref, wg_ref, bg_ref,
                    wp_ref, pp_ref, ps_ref, *rest):
    n_out = 2 if split_out else 1
    out_refs = rest[:n_out]
    xs_ref, ws_ref, perm_ref, flag_v, flag_s, sem = rest[n_out:]
    ys_ref = xs_ref
    is_sample = pl.program_id(0) >= N_PROMPT_MOE_BLOCKS
    xb = x1_ref[...].astype(_BF16)
    logits = lax.dot_general(wr_ref[...], xb, (((1,), (1,)), ((), ())), preferred_element_type=_F32)
    i1, i2, w1, w2 = _route(logits + br_ref[...])

    expert_row = lax.broadcasted_iota(_I32, (N_EXPERTS, SUB), 0).astype(_F32) + N_GROUPS
    r = lax.broadcasted_iota(_I32, (SUB, SUB), 0)
    c = lax.broadcasted_iota(_I32, (SUB, SUB), 1)
    earlier = jnp.where(r < c, 1.0, 0.0).astype(_BF16)
    sorted_row = lax.broadcasted_iota(_I32, (SUB_SORTED, SUB), 0).astype(_F32)
    hits = []
    fullest = jnp.zeros((1, 1), _F32)
    for s in range(N_SUB):
        rows = slice(s * SUB, (s + 1) * SUB)
        hit1 = expert_row == i1[:, rows]
        hit2 = expert_row == i2[:, rows]
        onehot = jnp.where(jnp.logical_or(hit1, hit2), 1.0, 0.0)
        fullest = jnp.maximum(fullest, jnp.max(jnp.sum(onehot, axis=1, keepdims=True), axis=0, keepdims=True))
        hits.append((hit1, hit2, onehot))
    flag_v[...] = jnp.broadcast_to(fullest, flag_v.shape).astype(_I32)
    to_smem = pltpu.make_async_copy(flag_v, flag_s, sem)
    to_smem.start()

    for s in range(N_SUB):
        rows = slice(s * SUB, (s + 1) * SUB)
        hit1, hit2, onehot = hits[s]
        rank = _dot(onehot.astype(_BF16), earlier)
        q1 = (i1[:, rows] - N_GROUPS) * SLOTS + jnp.sum(jnp.where(hit1, rank, 0.0), axis=0, keepdims=True)
        q2 = (i2[:, rows] - N_GROUPS) * SLOTS + jnp.sum(jnp.where(hit2, rank, 0.0), axis=0, keepdims=True)
        at1 = sorted_row == q1
        at2 = sorted_row == q2
        perm = jnp.where(jnp.logical_or(at1, at2), 1.0, 0.0).astype(_BF16)
        perm_ref[s] = perm
        xs_ref[s] = _dot(perm, xb[rows]).astype(_BF16)
        weight = jnp.sum(jnp.where(at1, w1[:, rows], 0.0) + jnp.where(at2, w2[:, rows], 0.0),
                         axis=-1, keepdims=True)
        ws_ref[s] = jnp.broadcast_to(weight, (SUB_SORTED, V7X_LANES))

    to_smem.wait()
    fits = flag_s[0, 0] <= SLOTS

    def finish(rows, moe):
        x2 = _layer_norm(DEEPNORM_ALPHA * x1_ref[rows, :] + moe, g2_ref[...], b2_ref[...])
        gate = jax.nn.sigmoid(_dot(x2.astype(_BF16), wg_ref[...]) + bg_ref[...])
        p = jnp.where(is_sample, ps_ref[0, rows, :], pp_ref[0, rows, :])
        out = x2 + gate * _dot(p.astype(_BF16), wp_ref[...])
        if split_out:
            @pl.when(jnp.logical_not(is_sample))
            def _():
                out_refs[0][rows, :] = out

            @pl.when(is_sample)
            def _():
                out_refs[1][rows, :] = out
        else:
            out_refs[0][rows, :] = out

    @pl.when(fits)
    def _():
        for e in range(N_EXPERTS):
            slots = slice(e * SLOTS, (e + 1) * SLOTS)
            xt = jnp.concatenate([xs_ref[s, slots, :] for s in range(N_SUB)], axis=0)
            wt = jnp.concatenate([ws_ref[s, slots, 0:1] for s in range(N_SUB)], axis=0)
            y = _dot(_expert_act(_dot(xt, wgu_ref[e]), wt), wd_ref[e]).astype(_BF16)
            for s in range(N_SUB):
                ys_ref[s, slots, :] = y[s * SLOTS:(s + 1) * SLOTS, :]
        for s in range(N_SUB):
            moe = lax.dot_general(perm_ref[s], ys_ref[s], (((0,), (0,)), ((), ())), preferred_element_type=_F32)
            finish(slice(s * SUB, (s + 1) * SUB), moe)

    @pl.when(jnp.logical_not(fits))
    def _():
        row_all = lax.broadcasted_iota(_I32, (V7X_LANES, MOE_TM), 0).astype(_F32)
        combine = (jnp.where(row_all == i1, w1, 0.0) + jnp.where(row_all == i2, w2, 0.0)).T
        acc = jnp.zeros((MOE_TM, D_MODEL), _F32)
        for e in range(N_EXPERTS):
            w = combine[:, N_GROUPS + e:N_GROUPS + e + 1]
            acc = acc + _dot(_expert_act(_dot(xb, wgu_ref[e]), w), wd_ref[e])
        finish(slice(0, MOE_TM), acc)


def _moe_ple(layer, split_out, x1, wr, br, wgu, wd, g2, b2, wg, bg, wp, p_prompt, p_sample):
    prompt_block = lambda i: jnp.minimum(i, N_PROMPT_MOE_BLOCKS - 1)
    sample_block = lambda i: jnp.maximum(i - N_PROMPT_MOE_BLOCKS, 0)
    if split_out:
        out_specs = [pl.BlockSpec((MOE_TM, D_MODEL), lambda i: (prompt_block(i), 0)),
                     pl.BlockSpec((MOE_TM, D_MODEL), lambda i: (sample_block(i), 0))]
        out_shape = [jax.ShapeDtypeStruct((N_PROMPT_TOK, D_MODEL), _F32),
                     jax.ShapeDtypeStruct((N_SAMPLE_TOK, D_MODEL), _F32)]
    else:
        out_specs = [pl.BlockSpec((MOE_TM, D_MODEL), lambda i: (i, 0))]
        out_shape = [jax.ShapeDtypeStruct((N_TOK, D_MODEL), _F32)]
    return pl.pallas_call(
        functools.partial(_moe_ple_kernel, split_out),
        grid=(N_TOK // MOE_TM,),
        in_specs=[
            pl.BlockSpec((MOE_TM, D_MODEL), lambda i: (i, 0)),
            _layer_spec(layer, (ROUTER_ROWS, D_MODEL)),
            _layer_spec(layer, (ROUTER_ROWS, 1)),
            _layer_spec(layer, (N_EXPERTS, D_MODEL, 2 * D_EXPERT)),
            _layer_spec(layer, (N_EXPERTS, D_EXPERT, D_MODEL)),
            _const_spec((1, D_MODEL)),
            _const_spec((1, D_MODEL)),
            _layer_spec(layer, (D_MODEL, D_MODEL)),
            _const_spec((1, D_MODEL)),
            _layer_spec(layer, (PLE_DIM, D_MODEL)),
            pl.BlockSpec((1, MOE_TM, PLE_DIM), lambda i: (layer, prompt_block(i), 0)),
            pl.BlockSpec((1, MOE_TM, PLE_DIM), lambda i: (layer, sample_block(i), 0)),
        ],
        out_specs=out_specs,
        out_shape=out_shape,
        scratch_shapes=[
            pltpu.VMEM((N_SUB, SUB_SORTED, D_MODEL), _BF16),
            pltpu.VMEM((N_SUB, SUB_SORTED, V7X_LANES), _F32),
            pltpu.VMEM((N_SUB, SUB_SORTED, SUB), _BF16),
            pltpu.VMEM((V7X_SUBLANES, V7X_LANES), _I32),
            pltpu.SMEM((V7X_SUBLANES, V7X_LANES), _I32),
            pltpu.SemaphoreType.DMA,
        ],
        compiler_params=_PARAMS,
        name="moe_ple",
    )(x1, wr, br, wgu, wd, g2, b2, wg, bg, wp, p_prompt, p_sample)


def _row(v):
    return v.reshape(1, -1)


def _pad_lanes(a, lanes):
    return jnp.pad(a, [(0, 0)] * (a.ndim - 1) + [(0, lanes - a.shape[-1])])


def _pad_rows(a):
    return jnp.pad(a, [(0, V7X_SUBLANES - a.shape[0]), (0, 0)])


def kernel(x_prompt, x_sample, state_conv, p_prompt, p_sample, a_w_in, a_b_in, a_ln_g, a_ln_b, a_w_s, a_b_s, a_w_out, b_w_in, b_conv_w, b_w_out, ln1_g, ln1_b, ln2_g, ln2_b, moe_w_group, moe_b_group, moe_w_expert, moe_b_expert, moe_w_gate_up, moe_w_down, ple_w_gate, ple_b_gate, ple_w_proj):
    xs = (x_prompt.reshape(N_PROMPT_TOK, D_MODEL), x_sample.reshape(N_SAMPLE_TOK, D_MODEL))
    pp = p_prompt.reshape(DEPTH, N_PROMPT_TOK, PLE_DIM)
    ps = p_sample.reshape(DEPTH, N_SAMPLE_TOK, PLE_DIM)

    reps = CHUNK // DEC_SEQ
    ws_sample = jnp.tile(a_w_s[:, :, :DEC_SEQ, :DEC_SEQ], (1, 1, reps, reps))
    ws2 = jnp.stack([a_w_s, ws_sample], axis=1)
    bs_sample = jnp.tile(a_b_s[:, :, :DEC_SEQ], (1, 1, reps))
    bs2 = _pad_lanes(jnp.swapaxes(jnp.stack([a_b_s, bs_sample], axis=1), -1, -2), V7X_LANES)

    pos = lax.broadcasted_iota(_I32, (1, 1, DEC_SEQ, 1), 2)
    older, newer = state_conv[:, :, 0:1], state_conv[:, :, 1:2]
    past1 = jnp.where(pos == 0, newer, 0.0).reshape(-1, N_SAMPLE_TOK, B_WIDTH)
    past2 = jnp.where(pos == 0, older, jnp.where(pos == 1, newer, 0.0)).reshape(-1, N_SAMPLE_TOK, B_WIDTH)

    w_router = jnp.swapaxes(_pad_lanes(jnp.concatenate([moe_w_group, moe_w_expert], axis=-1), ROUTER_ROWS),
                            -1, -2).astype(_BF16)
    b_router = _pad_lanes(jnp.concatenate([moe_b_group, moe_b_expert], axis=-1), ROUTER_ROWS)[..., None]

    a_w_in, a_w_out, b_w_in, b_w_out, moe_w_gate_up, moe_w_down, ple_w_gate, ple_w_proj = (
        w.astype(_BF16) for w in (a_w_in, a_w_out, b_w_in, b_w_out, moe_w_gate_up, moe_w_down,
                                  ple_w_gate, ple_w_proj))

    v_prompt, v_sample, st_prompt, st_sample = [], [], [], []
    for i in range(DEPTH):
        j = i // 2
        if i % 2 == 0:
            x1, vp, vs = _mixer_a(j, xs, a_w_in, _row(a_b_in[j]), _row(a_ln_g[j]), _row(a_ln_b[j]),
                                  ws2[j], bs2[j], a_w_out, _row(ln1_g[i]), _row(ln1_b[i]))
            v_prompt.append(vp)
            v_sample.append(vs)
        else:
            x1, stp, sts = _mixer_b(j, xs[0], b_w_in, _pad_rows(b_conv_w[j]), b_w_out,
                                    _row(ln1_g[i]), _row(ln1_b[i]), past1[j], past2[j])
            st_prompt.append(stp)
            st_sample.append(sts)
        xs = _moe_ple(i, i == DEPTH - 1, x1, w_router, b_router, moe_w_gate_up, moe_w_down,
                      _row(ln2_g[i]), _row(ln2_b[i]), ple_w_gate, _row(ple_b_gate[i]), ple_w_proj, pp, ps)

    keep = CONV_WIDTH - 1
    return (xs[0].reshape(BATCH, SEQ, D_MODEL),
            xs[1].reshape(DEC_BATCH, DEC_SEQ, D_MODEL),
            jnp.stack(v_prompt).reshape(-1, BATCH, CHUNK, A_WIDTH),
            jnp.stack(v_sample).reshape(-1, DEC_BATCH, DEC_SEQ, A_WIDTH),
            jnp.stack(st_prompt).reshape(-1, BATCH, V7X_SUBLANES, B_WIDTH)[:, :, V7X_SUBLANES - keep:],
            jnp.stack(st_sample).reshape(-1, DEC_BATCH, DEC_SEQ, B_WIDTH)[:, :, DEC_SEQ - keep:])
```

```python
import functools
import math

import jax
import jax.numpy as jnp
from jax import lax
from jax.experimental import pallas as pl
from jax.experimental.pallas import tpu as pltpu

D_MODEL = 1024
BATCH = 8
SEQ = 2048
DEPTH = 4
DEC_BATCH = 128
DEC_SEQ = 8
CHUNK = 128
A_HEADS = 4
A_WIDTH = 3 * D_MODEL
A_HEAD_DIM = A_WIDTH // A_HEADS
CONV_WIDTH = 3
B_WIDTH = D_MODEL
N_GROUPS = 4
EXPERTS_PER_GROUP = 4
N_EXPERTS = N_GROUPS * EXPERTS_PER_GROUP
D_EXPERT = D_MODEL // 4
PLE_DIM = 256
LN_EPS = 1e-5
DEEPNORM_ALPHA = (2 * DEPTH) ** 0.25

V7X_LANES = 128
V7X_SUBLANES = 8
V7X_VMEM_BYTES = 64 * 1024 * 1024

N_PROMPT_TOK = BATCH * SEQ
N_SAMPLE_TOK = DEC_BATCH * DEC_SEQ
N_TOK = N_PROMPT_TOK + N_SAMPLE_TOK
TM = 512
CHAIN_ROWS = 256
N_PROMPT_BLOCKS = N_PROMPT_TOK // TM
N_SAMPLE_BLOCKS = N_SAMPLE_TOK // TM
N_BLOCKS = N_PROMPT_BLOCKS + N_SAMPLE_BLOCKS
BLOCKS_PER_SEQ = SEQ // TM
ROUTER_ROWS = 32
VMEM_LIMIT = V7X_VMEM_BYTES - 4 * 1024 * 1024

MOE_TM = 512
SUB = 256
N_SUB = MOE_TM // SUB
SLOTS = 64
SUB_SORTED = N_EXPERTS * SLOTS
N_PROMPT_MOE_BLOCKS = N_PROMPT_TOK // MOE_TM

assert N_PROMPT_TOK % MOE_TM == 0 and N_SAMPLE_TOK % MOE_TM == 0 and MOE_TM % SUB == 0
assert SEQ % TM == 0 and N_SAMPLE_TOK % TM == 0 and TM % CHUNK == 0 and TM % DEC_SEQ == 0
assert CHUNK % DEC_SEQ == 0 and CONV_WIDTH - 1 <= V7X_SUBLANES

_BF16 = jnp.bfloat16
_F32 = jnp.float32
_I32 = jnp.int32


def _layer_norm(x, g, b):
    mu = jnp.mean(x, axis=-1, keepdims=True)
    xc = x - mu
    var = jnp.mean(xc * xc, axis=-1, keepdims=True)
    return xc * lax.rsqrt(var + LN_EPS) * g + b


def _gelu(x):
    h = 0.5 * x
    return h + h * lax.erf(x * math.sqrt(0.5))


def _dot(a, b):
    return jnp.dot(a, b, preferred_element_type=_F32)


def _const_spec(shape):
    zeros = (0,) * len(shape)
    return pl.BlockSpec(shape, lambda i: zeros, pipeline_mode=pl.Buffered(1))


def _layer_spec(layer, shape):
    zeros = (0,) * len(shape)
    return pl.BlockSpec((None,) + tuple(shape), lambda i: (layer,) + zeros, pipeline_mode=pl.Buffered(1))


def _tok_spec(width):
    return pl.BlockSpec((TM, width), lambda i: (i, 0))


def _prompt_block(i):
    return jnp.minimum(i, N_PROMPT_BLOCKS - 1)


def _sample_block(i):
    return jnp.maximum(i - N_PROMPT_BLOCKS, 0)


def _prompt_spec(width):
    return pl.BlockSpec((TM, width), lambda i: (_prompt_block(i), 0))


def _sample_spec(width):
    return pl.BlockSpec((TM, width), lambda i: (_sample_block(i), 0))


def _prompt_state_spec(rows, width):
    return pl.BlockSpec((rows, width), lambda i: (_prompt_block(i) // BLOCKS_PER_SEQ, 0))


_PARAMS = pltpu.CompilerParams(dimension_semantics=("arbitrary",), vmem_limit_bytes=VMEM_LIMIT)


def _is_sample():
    return pl.program_id(0) >= N_PROMPT_BLOCKS


def _read_tokens(split, refs):
    if not split:
        return refs[0][...]
    return jnp.where(_is_sample(), refs[1][...], refs[0][...])


def _write_by_side(prompt_ref, sample_ref, prompt_val, sample_val):
    @pl.when(jnp.logical_not(_is_sample()))
    def _():
        prompt_ref[...] = prompt_val

    @pl.when(_is_sample())
    def _():
        sample_ref[...] = sample_val


def _mixer_a_kernel(split_in, *refs):
    n_x = 2 if split_in else 1
    x_refs = refs[:n_x]
    (w_in_ref, b_in_ref, lng_ref, lnb_ref, ws_ref, bs_ref, w_out_ref, g1_ref, b1_ref,
     x1_ref, vp_ref, vs_ref, t_ref) = refs[n_x:]
    r = lax.broadcasted_iota(_I32, (CHUNK, CHUNK), 0)
    c = lax.broadcasted_iota(_I32, (CHUNK, CHUNK), 1)
    shift = jnp.where(_is_sample(), int(math.log2(DEC_SEQ)), int(math.log2(CHUNK)))
    mask = jnp.logical_and(r >= c, (r >> shift) == (c >> shift))
    w_spatial = [jnp.where(mask, ws_ref[0, h], 0.0).astype(_BF16) for h in range(A_HEADS)]

    n_chains = TM // CHAIN_ROWS
    rows = [slice(k * CHAIN_ROWS, (k + 1) * CHAIN_ROWS) for k in range(n_chains)]
    st = [dict() for _ in range(n_chains)]

    def proj_v(k):
        if split_in:
            x = jnp.where(_is_sample(), x_refs[1][rows[k], :], x_refs[0][rows[k], :])
        else:
            x = x_refs[0][rows[k], :]
        st[k]["x"] = x
        st[k]["xb"] = x.astype(_BF16)
        st[k]["zv"] = _dot(st[k]["xb"], w_in_ref[:, A_WIDTH:]) + b_in_ref[:, A_WIDTH:]

    def proj_u(k):
        st[k]["zu"] = _dot(st[k]["xb"], w_in_ref[:, :A_WIDTH]) + b_in_ref[:, :A_WIDTH]

    def act_v(k):
        v = _layer_norm(_gelu(st[k].pop("zv")), lng_ref[...], lnb_ref[...])
        vs_ref[rows[k], :] = v
        st[k]["vb"] = v.astype(_BF16)

    def act_u(k):
        st[k]["u"] = _gelu(st[k].pop("zu"))

    def gate(k):
        u, vb = st[k].pop("u"), st[k].pop("vb")
        for h in range(A_HEADS):
            cols = slice(h * A_HEAD_DIM, (h + 1) * A_HEAD_DIM)
            for q in range(CHAIN_ROWS // CHUNK):
                chunk = slice(q * CHUNK, (q + 1) * CHUNK)
                sg = _dot(w_spatial[h], vb[chunk, cols]) + bs_ref[0, :, h:h + 1]
                t_ref[k * CHAIN_ROWS + q * CHUNK:k * CHAIN_ROWS + (q + 1) * CHUNK, cols] = (
                    u[chunk, cols] * sg).astype(_BF16)

    def out(k):
        y = _dot(t_ref[rows[k], :], w_out_ref[...])
        x1_ref[rows[k], :] = _layer_norm(DEEPNORM_ALPHA * st[k].pop("x") + y, g1_ref[...], b1_ref[...])

    for k in range(n_chains):
        for stage in (proj_v, act_v, proj_u, act_u, gate, out):
            stage(k)

    @pl.when(jnp.logical_not(_is_sample()))
    def _():
        vp_ref[...] = vs_ref[TM - CHUNK:, :]


def _mixer_a(j, xs, w_in, b_in, ln_g, ln_b, ws2, bs2, w_out, g1, b1):
    split_in = len(xs) == 2
    x_specs = [_prompt_spec(D_MODEL), _sample_spec(D_MODEL)] if split_in else [_tok_spec(D_MODEL)]
    side = lambda i: jnp.where(i < N_PROMPT_BLOCKS, 0, 1)
    return pl.pallas_call(
        functools.partial(_mixer_a_kernel, split_in),
        grid=(N_BLOCKS,),
        in_specs=x_specs + [
            _layer_spec(j, (D_MODEL, 2 * A_WIDTH)),
            _const_spec((1, 2 * A_WIDTH)),
            _const_spec((1, A_WIDTH)),
            _const_spec((1, A_WIDTH)),
            pl.BlockSpec((1, A_HEADS, CHUNK, CHUNK), lambda i: (side(i), 0, 0, 0)),
            pl.BlockSpec((1, CHUNK, V7X_LANES), lambda i: (side(i), 0, 0)),
            _layer_spec(j, (A_WIDTH, D_MODEL)),
            _const_spec((1, D_MODEL)),
            _const_spec((1, D_MODEL)),
        ],
        out_specs=[
            _tok_spec(D_MODEL),
            _prompt_state_spec(CHUNK, A_WIDTH),
            _sample_spec(A_WIDTH),
        ],
        out_shape=[
            jax.ShapeDtypeStruct((N_TOK, D_MODEL), _F32),
            jax.ShapeDtypeStruct((BATCH * CHUNK, A_WIDTH), _F32),
            jax.ShapeDtypeStruct((N_SAMPLE_TOK, A_WIDTH), _F32),
        ],
        scratch_shapes=[pltpu.VMEM((TM, A_WIDTH), _BF16)],
        compiler_params=_PARAMS,
        name="mixer_a",
    )(*xs, w_in, b_in, ln_g, ln_b, ws2, bs2, w_out, g1, b1)


def _mixer_b_kernel(x_ref, w_in_ref, cw_ref, w_out_ref, g1_ref, b1_ref, past1_ref, past2_ref,
                    x1_ref, stp_ref, sts_ref, ext_ref):
    i = pl.program_id(0)
    is_sample = _is_sample()
    @pl.when(jnp.logical_or(is_sample, i % BLOCKS_PER_SEQ == 0))
    def _():
        ext_ref[0:V7X_SUBLANES, :] = jnp.zeros((V7X_SUBLANES, B_WIDTH), _F32)

    pos = lax.broadcasted_iota(_I32, (TM, B_WIDTH), 0) % DEC_SEQ
    pos = jnp.where(is_sample, pos, DEC_SEQ)
    x = x_ref[...]
    h = _dot(x.astype(_BF16), w_in_ref[...])
    gate_b = h[:, :B_WIDTH]
    p = h[:, B_WIDTH:2 * B_WIDTH] * h[:, 2 * B_WIDTH:]
    ext_ref[V7X_SUBLANES:, :] = p
    prev1 = jnp.where(pos < 1, past1_ref[...], ext_ref[V7X_SUBLANES - 1:V7X_SUBLANES - 1 + TM, :])
    prev2 = jnp.where(pos < 2, past2_ref[...], ext_ref[V7X_SUBLANES - 2:V7X_SUBLANES - 2 + TM, :])
    conv = cw_ref[0:1, :] * prev2 + cw_ref[1:2, :] * prev1 + cw_ref[2:3, :] * p
    y = _dot((gate_b * conv).astype(_BF16), w_out_ref[...])
    x1_ref[...] = _layer_norm(DEEPNORM_ALPHA * x + y, g1_ref[...], b1_ref[...])
    ext_ref[0:V7X_SUBLANES, :] = ext_ref[TM:TM + V7X_SUBLANES, :]

    @pl.when(jnp.logical_not(is_sample))
    def _():
        stp_ref[...] = ext_ref[TM:TM + V7X_SUBLANES, :]

    @pl.when(is_sample)
    def _():
        sts_ref[...] = ext_ref[V7X_SUBLANES:, :]


def _mixer_b(j, x, w_in, conv_w, w_out, g1, b1, past1, past2):
    return pl.pallas_call(
        _mixer_b_kernel,
        grid=(N_BLOCKS,),
        in_specs=[
            _tok_spec(D_MODEL),
            _layer_spec(j, (D_MODEL, 3 * B_WIDTH)),
            _const_spec((V7X_SUBLANES, B_WIDTH)),
            _layer_spec(j, (B_WIDTH, D_MODEL)),
            _const_spec((1, D_MODEL)),
            _const_spec((1, D_MODEL)),
            _sample_spec(B_WIDTH),
            _sample_spec(B_WIDTH),
        ],
        out_specs=[
            _tok_spec(D_MODEL),
            _prompt_state_spec(V7X_SUBLANES, B_WIDTH),
            _sample_spec(B_WIDTH),
        ],
        out_shape=[
            jax.ShapeDtypeStruct((N_TOK, D_MODEL), _F32),
            jax.ShapeDtypeStruct((BATCH * V7X_SUBLANES, B_WIDTH), _F32),
            jax.ShapeDtypeStruct((N_SAMPLE_TOK, B_WIDTH), _F32),
        ],
        scratch_shapes=[pltpu.VMEM((V7X_SUBLANES + TM, B_WIDTH), _F32)],
        compiler_params=_PARAMS,
        name="mixer_b",
    )(x, w_in, conv_w, w_out, g1, b1, past1, past2)


def _route(logits):
    row = lax.broadcasted_iota(_I32, logits.shape, 0)
    row_f = row.astype(_F32)
    neg = jnp.float32(-jnp.inf)
    far = jnp.float32(ROUTER_ROWS)

    def first_argmax(vals):
        top = jnp.max(vals, axis=0, keepdims=True)
        idx = jnp.min(jnp.where(vals == top, row_f, far), axis=0, keepdims=True)
        return top, idx

    is_group = row < N_GROUPS
    g_top, g_idx = first_argmax(jnp.where(is_group, logits, neg))
    g_w = 1.0 / jnp.sum(jnp.where(is_group, jnp.exp(logits - g_top), 0.0), axis=0, keepdims=True)
    lo = N_GROUPS + EXPERTS_PER_GROUP * g_idx
    in_group = jnp.logical_and(row_f >= lo, row_f < lo + EXPERTS_PER_GROUP)
    e_logits = jnp.where(in_group, logits, neg)
    v1, i1 = first_argmax(e_logits)
    v2, i2 = first_argmax(jnp.where(row_f == i1, neg, e_logits))
    e2 = jnp.exp(v2 - v1)
    return i1, i2, g_w / (1.0 + e2), g_w * e2 / (1.0 + e2)


def _expert_act(hgu, scale):
    gate = hgu[:, :D_EXPERT]
    return (gate * jax.nn.sigmoid(gate) * hgu[:, D_EXPERT:] * scale).astype(_BF16)


def _moe_ple_kernel(split_out, x1_ref, wr_ref, br_ref, wgu_ref, wd_ref, g2_ref, b2_ref, wg_ref, bg_ref,
                    wp_ref, pp_ref, ps_ref, *rest):
    n_out = 2 if split_out else 1
    out_refs = rest[:n_out]
    xs_ref, ws_ref, perm_ref, flag_v, flag_s, sem = rest[n_out:]
    ys_ref = xs_ref
    is_sample = pl.program_id(0) >= N_PROMPT_MOE_BLOCKS
    xb = x1_ref[...].astype(_BF16)
    logits = lax.dot_general(wr_ref[...], xb, (((1,), (1,)), ((), ())), preferred_element_type=_F32)
    i1, i2, w1, w2 = _route(logits + br_ref[...])

    expert_row = lax.broadcasted_iota(_I32, (N_EXPERTS, SUB), 0).astype(_F32) + N_GROUPS
    r = lax.broadcasted_iota(_I32, (SUB, SUB), 0)
    c = lax.broadcasted_iota(_I32, (SUB, SUB), 1)
    earlier = jnp.where(r < c, 1.0, 0.0).astype(_BF16)
    sorted_row = lax.broadcasted_iota(_I32, (SUB_SORTED, SUB), 0).astype(_F32)
    hits = []
    fullest = jnp.zeros((1, 1), _F32)
    for s in range(N_SUB):
        rows = slice(s * SUB, (s + 1) * SUB)
        hit1 = expert_row == i1[:, rows]
        hit2 = expert_row == i2[:, rows]
        onehot = jnp.where(jnp.logical_or(hit1, hit2), 1.0, 0.0)
        fullest = jnp.maximum(fullest, jnp.max(jnp.sum(onehot, axis=1, keepdims=True), axis=0, keepdims=True))
        hits.append((hit1, hit2, onehot))
    flag_v[...] = jnp.broadcast_to(fullest, flag_v.shape).astype(_I32)
    to_smem = pltpu.make_async_copy(flag_v, flag_s, sem)
    to_smem.start()

    for s in range(N_SUB):
        rows = slice(s * SUB, (s + 1) * SUB)
        hit1, hit2, onehot = hits[s]
        rank = _dot(onehot.astype(_BF16), earlier)
        q1 = (i1[:, rows] - N_GROUPS) * SLOTS + jnp.sum(jnp.where(hit1, rank, 0.0), axis=0, keepdims=True)
        q2 = (i2[:, rows] - N_GROUPS) * SLOTS + jnp.sum(jnp.where(hit2, rank, 0.0), axis=0, keepdims=True)
        at1 = sorted_row == q1
        at2 = sorted_row == q2
        perm = jnp.where(jnp.logical_or(at1, at2), 1.0, 0.0).astype(_BF16)
        perm_ref[s] = perm
        xs_ref[s] = _dot(perm, xb[rows]).astype(_BF16)
        weight = jnp.sum(jnp.where(at1, w1[:, rows], 0.0) + jnp.where(at2, w2[:, rows], 0.0),
                         axis=-1, keepdims=True)
        ws_ref[s] = jnp.broadcast_to(weight, (SUB_SORTED, V7X_LANES))

    to_smem.wait()
    fits = flag_s[0, 0] <= SLOTS

    def finish(rows, moe):
        x2 = _layer_norm(DEEPNORM_ALPHA * x1_ref[rows, :] + moe, g2_ref[...], b2_ref[...])
        gate = jax.nn.sigmoid(_dot(x2.astype(_BF16), wg_ref[...]) + bg_ref[...])
        p = jnp.where(is_sample, ps_ref[0, rows, :], pp_ref[0, rows, :])
        out = x2 + gate * _dot(p.astype(_BF16), wp_ref[...])
        if split_out:
            @pl.when(jnp.logical_not(is_sample))
            def _():
                out_refs[0][rows, :] = out

            @pl.when(is_sample)
            def _():
                out_refs[1][rows, :] = out
        else:
            out_refs[0][rows, :] = out

    @pl.when(fits)
    def _():
        def slots_of(e):
            return slice(e * SLOTS, (e + 1) * SLOTS)

        def up(e):
            xt = jnp.concatenate([xs_ref[s, slots_of(e), :] for s in range(N_SUB)], axis=0)
            return _dot(xt, wgu_ref[e])

        hgu = up(0)
        for e in range(N_EXPERTS):
            hgu_next = up(e + 1) if e + 1 < N_EXPERTS else None
            wt = jnp.concatenate([ws_ref[s, slots_of(e), 0:1] for s in range(N_SUB)], axis=0)
            y = _dot(_expert_act(hgu, wt), wd_ref[e]).astype(_BF16)
            for s in range(N_SUB):
                ys_ref[s, slots_of(e), :] = y[s * SLOTS:(s + 1) * SLOTS, :]
            hgu = hgu_next
        moe = [lax.dot_general(perm_ref[s], ys_ref[s], (((0,), (0,)), ((), ())), preferred_element_type=_F32)
               for s in range(N_SUB)]
        for s in range(N_SUB):
            finish(slice(s * SUB, (s + 1) * SUB), moe[s])

    @pl.when(jnp.logical_not(fits))
    def _():
        row_all = lax.broadcasted_iota(_I32, (V7X_LANES, MOE_TM), 0).astype(_F32)
        combine = (jnp.where(row_all == i1, w1, 0.0) + jnp.where(row_all == i2, w2, 0.0)).T
        acc = jnp.zeros((MOE_TM, D_MODEL), _F32)
        for e in range(N_EXPERTS):
            w = combine[:, N_GROUPS + e:N_GROUPS + e + 1]
            acc = acc + _dot(_expert_act(_dot(xb, wgu_ref[e]), w), wd_ref[e])
        finish(slice(0, MOE_TM), acc)


def _moe_ple(layer, split_out, x1, wr, br, wgu, wd, g2, b2, wg, bg, wp, p_prompt, p_sample):
    prompt_block = lambda i: jnp.minimum(i, N_PROMPT_MOE_BLOCKS - 1)
    sample_block = lambda i: jnp.maximum(i - N_PROMPT_MOE_BLOCKS, 0)
    if split_out:
        out_specs = [pl.BlockSpec((MOE_TM, D_MODEL), lambda i: (prompt_block(i), 0)),
                     pl.BlockSpec((MOE_TM, D_MODEL), lambda i: (sample_block(i), 0))]
        out_shape = [jax.ShapeDtypeStruct((N_PROMPT_TOK, D_MODEL), _F32),
                     jax.ShapeDtypeStruct((N_SAMPLE_TOK, D_MODEL), _F32)]
    else:
        out_specs = [pl.BlockSpec((MOE_TM, D_MODEL), lambda i: (i, 0))]
        out_shape = [jax.ShapeDtypeStruct((N_TOK, D_MODEL), _F32)]
    return pl.pallas_call(
        functools.partial(_moe_ple_kernel, split_out),
        grid=(N_TOK // MOE_TM,),
        in_specs=[
            pl.BlockSpec((MOE_TM, D_MODEL), lambda i: (i, 0)),
            _layer_spec(layer, (ROUTER_ROWS, D_MODEL)),
            _layer_spec(layer, (ROUTER_ROWS, 1)),
            _layer_spec(layer, (N_EXPERTS, D_MODEL, 2 * D_EXPERT)),
            _layer_spec(layer, (N_EXPERTS, D_EXPERT, D_MODEL)),
            _const_spec((1, D_MODEL)),
            _const_spec((1, D_MODEL)),
            _layer_spec(layer, (D_MODEL, D_MODEL)),
            _const_spec((1, D_MODEL)),
            _layer_spec(layer, (PLE_DIM, D_MODEL)),
            pl.BlockSpec((1, MOE_TM, PLE_DIM), lambda i: (layer, prompt_block(i), 0)),
            pl.BlockSpec((1, MOE_TM, PLE_DIM), lambda i: (layer, sample_block(i), 0)),
        ],
        out_specs=out_specs,
        out_shape=out_shape,
        scratch_shapes=[
            pltpu.VMEM((N_SUB, SUB_SORTED, D_MODEL), _BF16),
            pltpu.VMEM((N_SUB, SUB_SORTED, V7X_LANES), _F32),
            pltpu.VMEM((N_SUB, SUB_SORTED, SUB), _BF16),
            pltpu.VMEM((V7X_SUBLANES, V7X_LANES), _I32),
            pltpu.SMEM((V7X_SUBLANES, V7X_LANES), _I32),
            pltpu.SemaphoreType.DMA,
        ],
        compiler_params=_PARAMS,
        name="moe_ple",
    )(x1, wr, br, wgu, wd, g2, b2, wg, bg, wp, p_prompt, p_sample)


def _row(v):
    return v.reshape(1, -1)


def _pad_lanes(a, lanes):
    return jnp.pad(a, [(0, 0)] * (a.ndim - 1) + [(0, lanes - a.shape[-1])])


def _pad_rows(a):
    return jnp.pad(a, [(0, V7X_SUBLANES - a.shape[0]), (0, 0)])


def kernel(x_prompt, x_sample, state_conv, p_prompt, p_sample, a_w_in, a_b_in, a_ln_g, a_ln_b, a_w_s, a_b_s, a_w_out, b_w_in, b_conv_w, b_w_out, ln1_g, ln1_b, ln2_g, ln2_b, moe_w_group, moe_b_group, moe_w_expert, moe_b_expert, moe_w_gate_up, moe_w_down, ple_w_gate, ple_b_gate, ple_w_proj):
    xs = (x_prompt.reshape(N_PROMPT_TOK, D_MODEL), x_sample.reshape(N_SAMPLE_TOK, D_MODEL))
    pp = p_prompt.reshape(DEPTH, N_PROMPT_TOK, PLE_DIM)
    ps = p_sample.reshape(DEPTH, N_SAMPLE_TOK, PLE_DIM)

    reps = CHUNK // DEC_SEQ
    ws_sample = jnp.tile(a_w_s[:, :, :DEC_SEQ, :DEC_SEQ], (1, 1, reps, reps))
    ws2 = jnp.stack([a_w_s, ws_sample], axis=1)
    bs_sample = jnp.tile(a_b_s[:, :, :DEC_SEQ], (1, 1, reps))
    bs2 = _pad_lanes(jnp.swapaxes(jnp.stack([a_b_s, bs_sample], axis=1), -1, -2), V7X_LANES)

    pos = lax.broadcasted_iota(_I32, (1, 1, DEC_SEQ, 1), 2)
    older, newer = state_conv[:, :, 0:1], state_conv[:, :, 1:2]
    past1 = jnp.where(pos == 0, newer, 0.0).reshape(-1, N_SAMPLE_TOK, B_WIDTH)
    past2 = jnp.where(pos == 0, older, jnp.where(pos == 1, newer, 0.0)).reshape(-1, N_SAMPLE_TOK, B_WIDTH)

    w_router = jnp.swapaxes(_pad_lanes(jnp.concatenate([moe_w_group, moe_w_expert], axis=-1), ROUTER_ROWS),
                            -1, -2).astype(_BF16)
    b_router = _pad_lanes(jnp.concatenate([moe_b_group, moe_b_expert], axis=-1), ROUTER_ROWS)[..., None]

    a_w_in, a_w_out, b_w_in, b_w_out, moe_w_gate_up, moe_w_down, ple_w_gate, ple_w_proj = (
        w.astype(_BF16) for w in (a_w_in, a_w_out, b_w_in, b_w_out, moe_w_gate_up, moe_w_down,
                                  ple_w_gate, ple_w_proj))

    v_prompt, v_sample, st_prompt, st_sample = [], [], [], []
    for i in range(DEPTH):
        j = i // 2
        if i % 2 == 0:
            x1, vp, vs = _mixer_a(j, xs, a_w_in, _row(a_b_in[j]), _row(a_ln_g[j]), _row(a_ln_b[j]),
                                  ws2[j], bs2[j], a_w_out, _row(ln1_g[i]), _row(ln1_b[i]))
            v_prompt.append(vp)
            v_sample.append(vs)
        else:
            x1, stp, sts = _mixer_b(j, xs[0], b_w_in, _pad_rows(b_conv_w[j]), b_w_out,
                                    _row(ln1_g[i]), _row(ln1_b[i]), past1[j], past2[j])
            st_prompt.append(stp)
            st_sample.append(sts)
        xs = _moe_ple(i, i == DEPTH - 1, x1, w_router, b_router, moe_w_gate_up, moe_w_down,
                      _row(ln2_g[i]), _row(ln2_b[i]), ple_w_gate, _row(ple_b_gate[i]), ple_w_proj, pp, ps)

    keep = CONV_WIDTH - 1
    return (xs[0].reshape(BATCH, SEQ, D_MODEL),
            xs[1].reshape(DEC_BATCH, DEC_SEQ, D_MODEL),
            jnp.stack(v_prompt).reshape(-1, BATCH, CHUNK, A_WIDTH),
            jnp.stack(v_sample).reshape(-1, DEC_BATCH, DEC_SEQ, A_WIDTH),
            jnp.stack(st_prompt).reshape(-1, BATCH, V7X_SUBLANES, B_WIDTH)[:, :, V7X_SUBLANES - keep:],
            jnp.stack(st_sample).reshape(-1, DEC_BATCH, DEC_SEQ, B_WIDTH)[:, :, DEC_SEQ - keep:])
```

```python
import functools
import math

import jax
import jax.numpy as jnp
from jax import lax
from jax.experimental import pallas as pl
from jax.experimental.pallas import tpu as pltpu

D_MODEL = 1024
BATCH = 8
SEQ = 2048
DEPTH = 4
DEC_BATCH = 128
DEC_SEQ = 8
CHUNK = 128
A_HEADS = 4
A_WIDTH = 3 * D_MODEL
A_HEAD_DIM = A_WIDTH // A_HEADS
CONV_WIDTH = 3
B_WIDTH = D_MODEL
N_GROUPS = 4
EXPERTS_PER_GROUP = 4
N_EXPERTS = N_GROUPS * EXPERTS_PER_GROUP
D_EXPERT = D_MODEL // 4
PLE_DIM = 256
LN_EPS = 1e-5
DEEPNORM_ALPHA = (2 * DEPTH) ** 0.25

V7X_LANES = 128
V7X_SUBLANES = 8
V7X_VMEM_BYTES = 64 * 1024 * 1024

N_PROMPT_TOK = BATCH * SEQ
N_SAMPLE_TOK = DEC_BATCH * DEC_SEQ
N_TOK = N_PROMPT_TOK + N_SAMPLE_TOK
TM = 512
CHAIN_ROWS = 256
N_PROMPT_BLOCKS = N_PROMPT_TOK // TM
N_SAMPLE_BLOCKS = N_SAMPLE_TOK // TM
N_BLOCKS = N_PROMPT_BLOCKS + N_SAMPLE_BLOCKS
BLOCKS_PER_SEQ = SEQ // TM
ROUTER_ROWS = 32
VMEM_LIMIT = V7X_VMEM_BYTES - 4 * 1024 * 1024

MOE_TM = 512
SUB = 256
N_SUB = MOE_TM // SUB
SLOTS = 64
SUB_SORTED = N_EXPERTS * SLOTS
N_PROMPT_MOE_BLOCKS = N_PROMPT_TOK // MOE_TM

assert N_PROMPT_TOK % MOE_TM == 0 and N_SAMPLE_TOK % MOE_TM == 0 and MOE_TM % SUB == 0
assert SEQ % TM == 0 and N_SAMPLE_TOK % TM == 0 and TM % CHUNK == 0 and TM % DEC_SEQ == 0
assert CHUNK % DEC_SEQ == 0 and CONV_WIDTH - 1 <= V7X_SUBLANES

_BF16 = jnp.bfloat16
_F32 = jnp.float32
_I32 = jnp.int32


def _layer_norm(x, g, b):
    mu = jnp.mean(x, axis=-1, keepdims=True)
    xc = x - mu
    var = jnp.mean(xc * xc, axis=-1, keepdims=True)
    return xc * lax.rsqrt(var + LN_EPS) * g + b


def _gelu(x):
    h = 0.5 * x
    return h + h * lax.erf(x * math.sqrt(0.5))


def _dot(a, b):
    return jnp.dot(a, b, preferred_element_type=_F32)


def _const_spec(shape):
    zeros = (0,) * len(shape)
    return pl.BlockSpec(shape, lambda i: zeros, pipeline_mode=pl.Buffered(1))


def _layer_spec(layer, shape):
    zeros = (0,) * len(shape)
    return pl.BlockSpec((None,) + tuple(shape), lambda i: (layer,) + zeros, pipeline_mode=pl.Buffered(1))


def _tok_spec(width):
    return pl.BlockSpec((TM, width), lambda i: (i, 0))


def _prompt_block(i):
    return jnp.minimum(i, N_PROMPT_BLOCKS - 1)


def _sample_block(i):
    return jnp.maximum(i - N_PROMPT_BLOCKS, 0)


def _prompt_spec(width):
    return pl.BlockSpec((TM, width), lambda i: (_prompt_block(i), 0))


def _sample_spec(width):
    return pl.BlockSpec((TM, width), lambda i: (_sample_block(i), 0))


def _prompt_state_spec(rows, width):
    return pl.BlockSpec((rows, width), lambda i: (_prompt_block(i) // BLOCKS_PER_SEQ, 0))


_PARAMS = pltpu.CompilerParams(dimension_semantics=("arbitrary",), vmem_limit_bytes=VMEM_LIMIT)


def _is_sample():
    return pl.program_id(0) >= N_PROMPT_BLOCKS


def _mixer_a_kernel(split_in, *refs):
    n_x = 2 if split_in else 1
    x_refs = refs[:n_x]
    (w_in_ref, b_in_ref, lng_ref, lnb_ref, ws_ref, bs_ref, w_out_ref, g1_ref, b1_ref,
     x1_ref, vp_ref, vs_ref, t_ref) = refs[n_x:]
    r = lax.broadcasted_iota(_I32, (CHUNK, CHUNK), 0)
    c = lax.broadcasted_iota(_I32, (CHUNK, CHUNK), 1)
    shift = jnp.where(_is_sample(), int(math.log2(DEC_SEQ)), int(math.log2(CHUNK)))
    mask = jnp.logical_and(r >= c, (r >> shift) == (c >> shift))
    w_spatial = [jnp.where(mask, ws_ref[0, h], 0.0).astype(_BF16) for h in range(A_HEADS)]

    n_chains = TM // CHAIN_ROWS
    rows = [slice(k * CHAIN_ROWS, (k + 1) * CHAIN_ROWS) for k in range(n_chains)]
    st = [dict() for _ in range(n_chains)]

    def proj_v(k):
        if split_in:
            x = jnp.where(_is_sample(), x_refs[1][rows[k], :], x_refs[0][rows[k], :])
        else:
            x = x_refs[0][rows[k], :]
        st[k]["x"] = x
        st[k]["xb"] = x.astype(_BF16)
        st[k]["zv"] = _dot(st[k]["xb"], w_in_ref[:, A_WIDTH:]) + b_in_ref[:, A_WIDTH:]

    def proj_u(k):
        st[k]["zu"] = _dot(st[k]["xb"], w_in_ref[:, :A_WIDTH]) + b_in_ref[:, :A_WIDTH]

    def act_v(k):
        v = _layer_norm(_gelu(st[k].pop("zv")), lng_ref[...], lnb_ref[...])
        vs_ref[rows[k], :] = v
        st[k]["vb"] = v.astype(_BF16)

    def act_u(k):
        st[k]["u"] = _gelu(st[k].pop("zu"))

    def gate(k):
        u, vb = st[k].pop("u"), st[k].pop("vb")
        for h in range(A_HEADS):
            cols = slice(h * A_HEAD_DIM, (h + 1) * A_HEAD_DIM)
            for q in range(CHAIN_ROWS // CHUNK):
                chunk = slice(q * CHUNK, (q + 1) * CHUNK)
                sg = _dot(w_spatial[h], vb[chunk, cols]) + bs_ref[0, :, h:h + 1]
                t_ref[k * CHAIN_ROWS + q * CHUNK:k * CHAIN_ROWS + (q + 1) * CHUNK, cols] = (
                    u[chunk, cols] * sg).astype(_BF16)

    def out(k):
        y = _dot(t_ref[rows[k], :], w_out_ref[...])
        x1_ref[rows[k], :] = _layer_norm(DEEPNORM_ALPHA * st[k].pop("x") + y, g1_ref[...], b1_ref[...])

    for k in range(n_chains):
        for stage in (proj_v, act_v, proj_u, act_u, gate, out):
            stage(k)

    @pl.when(jnp.logical_not(_is_sample()))
    def _():
        vp_ref[...] = vs_ref[TM - CHUNK:, :]


def _mixer_a(j, xs, w_in, b_in, ln_g, ln_b, ws2, bs2, w_out, g1, b1):
    split_in = len(xs) == 2
    x_specs = [_prompt_spec(D_MODEL), _sample_spec(D_MODEL)] if split_in else [_tok_spec(D_MODEL)]
    side = lambda i: jnp.where(i < N_PROMPT_BLOCKS, 0, 1)
    return pl.pallas_call(
        functools.partial(_mixer_a_kernel, split_in),
        grid=(N_BLOCKS,),
        in_specs=x_specs + [
            _layer_spec(j, (D_MODEL, 2 * A_WIDTH)),
            _const_spec((1, 2 * A_WIDTH)),
            _const_spec((1, A_WIDTH)),
            _const_spec((1, A_WIDTH)),
            pl.BlockSpec((1, A_HEADS, CHUNK, CHUNK), lambda i: (side(i), 0, 0, 0)),
            pl.BlockSpec((1, CHUNK, V7X_LANES), lambda i: (side(i), 0, 0)),
            _layer_spec(j, (A_WIDTH, D_MODEL)),
            _const_spec((1, D_MODEL)),
            _const_spec((1, D_MODEL)),
        ],
        out_specs=[
            _tok_spec(D_MODEL),
            _prompt_state_spec(CHUNK, A_WIDTH),
            _sample_spec(A_WIDTH),
        ],
        out_shape=[
            jax.ShapeDtypeStruct((N_TOK, D_MODEL), _F32),
            jax.ShapeDtypeStruct((BATCH * CHUNK, A_WIDTH), _F32),
            jax.ShapeDtypeStruct((N_SAMPLE_TOK, A_WIDTH), _F32),
        ],
        scratch_shapes=[pltpu.VMEM((TM, A_WIDTH), _BF16)],
        compiler_params=_PARAMS,
        name="mixer_a",
    )(*xs, w_in, b_in, ln_g, ln_b, ws2, bs2, w_out, g1, b1)


def _mixer_b_kernel(x_ref, w_in_ref, cw_ref, w_out_ref, g1_ref, b1_ref, past1_ref, past2_ref,
                    x1_ref, stp_ref, sts_ref, ext_ref):
    i = pl.program_id(0)
    is_sample = _is_sample()
    @pl.when(jnp.logical_or(is_sample, i % BLOCKS_PER_SEQ == 0))
    def _():
        ext_ref[0:V7X_SUBLANES, :] = jnp.zeros((V7X_SUBLANES, B_WIDTH), _F32)

    pos = lax.broadcasted_iota(_I32, (TM, B_WIDTH), 0) % DEC_SEQ
    pos = jnp.where(is_sample, pos, DEC_SEQ)
    x = x_ref[...]
    h = _dot(x.astype(_BF16), w_in_ref[...])
    gate_b = h[:, :B_WIDTH]
    p = h[:, B_WIDTH:2 * B_WIDTH] * h[:, 2 * B_WIDTH:]
    ext_ref[V7X_SUBLANES:, :] = p
    prev1 = jnp.where(pos < 1, past1_ref[...], ext_ref[V7X_SUBLANES - 1:V7X_SUBLANES - 1 + TM, :])
    prev2 = jnp.where(pos < 2, past2_ref[...], ext_ref[V7X_SUBLANES - 2:V7X_SUBLANES - 2 + TM, :])
    conv = cw_ref[0:1, :] * prev2 + cw_ref[1:2, :] * prev1 + cw_ref[2:3, :] * p
    y = _dot((gate_b * conv).astype(_BF16), w_out_ref[...])
    x1_ref[...] = _layer_norm(DEEPNORM_ALPHA * x + y, g1_ref[...], b1_ref[...])
    ext_ref[0:V7X_SUBLANES, :] = ext_ref[TM:TM + V7X_SUBLANES, :]

    @pl.when(jnp.logical_not(is_sample))
    def _():
        stp_ref[...] = ext_ref[TM:TM + V7X_SUBLANES, :]

    @pl.when(is_sample)
    def _():
        sts_ref[...] = ext_ref[V7X_SUBLANES:, :]


def _mixer_b(j, x, w_in, conv_w, w_out, g1, b1, past1, past2):
    return pl.pallas_call(
        _mixer_b_kernel,
        grid=(N_BLOCKS,),
        in_specs=[
            _tok_spec(D_MODEL),
            _layer_spec(j, (D_MODEL, 3 * B_WIDTH)),
            _const_spec((V7X_SUBLANES, B_WIDTH)),
            _layer_spec(j, (B_WIDTH, D_MODEL)),
            _const_spec((1, D_MODEL)),
            _const_spec((1, D_MODEL)),
            _sample_spec(B_WIDTH),
            _sample_spec(B_WIDTH),
        ],
        out_specs=[
            _tok_spec(D_MODEL),
            _prompt_state_spec(V7X_SUBLANES, B_WIDTH),
            _sample_spec(B_WIDTH),
        ],
        out_shape=[
            jax.ShapeDtypeStruct((N_TOK, D_MODEL), _F32),
            jax.ShapeDtypeStruct((BATCH * V7X_SUBLANES, B_WIDTH), _F32),
            jax.ShapeDtypeStruct((N_SAMPLE_TOK, B_WIDTH), _F32),
        ],
        scratch_shapes=[pltpu.VMEM((V7X_SUBLANES + TM, B_WIDTH), _F32)],
        compiler_params=_PARAMS,
        name="mixer_b",
    )(x, w_in, conv_w, w_out, g1, b1, past1, past2)


def _route(logits):
    row = lax.broadcasted_iota(_I32, logits.shape, 0)
    row_f = row.astype(_F32)
    neg = jnp.float32(-jnp.inf)
    far = jnp.float32(ROUTER_ROWS)

    def first_argmax(vals):
        top = jnp.max(vals, axis=0, keepdims=True)
        idx = jnp.min(jnp.where(vals == top, row_f, far), axis=0, keepdims=True)
        return top, idx

    is_group = row < N_GROUPS
    g_top, g_idx = first_argmax(jnp.where(is_group, logits, neg))
    g_w = 1.0 / jnp.sum(jnp.where(is_group, jnp.exp(logits - g_top), 0.0), axis=0, keepdims=True)
    lo = N_GROUPS + EXPERTS_PER_GROUP * g_idx
    in_group = jnp.logical_and(row_f >= lo, row_f < lo + EXPERTS_PER_GROUP)
    e_logits = jnp.where(in_group, logits, neg)
    v1, i1 = first_argmax(e_logits)
    v2, i2 = first_argmax(jnp.where(row_f == i1, neg, e_logits))
    e2 = jnp.exp(v2 - v1)
    return i1, i2, g_w / (1.0 + e2), g_w * e2 / (1.0 + e2)


def _expert_act(hgu, scale):
    gate = hgu[:, :D_EXPERT]
    return (gate * jax.nn.sigmoid(gate) * hgu[:, D_EXPERT:] * scale).astype(_BF16)


def _moe_ple_kernel(split_out, x1_ref, wr_ref, br_ref, wgu_ref, wd_ref, g2_ref, b2_ref, wg_ref, bg_ref,
                    wp_ref, pp_ref, ps_ref, *rest):
    n_out = 2 if split_out else 1
    out_refs = rest[:n_out]
    xs_ref, ws_ref, perm_ref, flag_v, flag_s, sem = rest[n_out:]
    ys_ref = xs_ref
    is_sample = pl.program_id(0) >= N_PROMPT_MOE_BLOCKS
    xb = x1_ref[...].astype(_BF16)
    logits = lax.dot_general(wr_ref[...], xb, (((1,), (1,)), ((), ())), preferred_element_type=_F32)
    i1, i2, w1, w2 = _route(logits + br_ref[...])

    expert_row = lax.broadcasted_iota(_I32, (N_EXPERTS, SUB), 0).astype(_F32) + N_GROUPS
    r = lax.broadcasted_iota(_I32, (SUB, SUB), 0)
    c = lax.broadcasted_iota(_I32, (SUB, SUB), 1)
    earlier = jnp.where(r < c, 1.0, 0.0).astype(_BF16)
    sorted_row = lax.broadcasted_iota(_I32, (SUB_SORTED, SUB), 0).astype(_F32)
    hits = []
    fullest = jnp.zeros((1, 1), _F32)
    for s in range(N_SUB):
        rows = slice(s * SUB, (s + 1) * SUB)
        hit1 = expert_row == i1[:, rows]
        hit2 = expert_row == i2[:, rows]
        onehot = jnp.where(jnp.logical_or(hit1, hit2), 1.0, 0.0)
        fullest = jnp.maximum(fullest, jnp.max(jnp.sum(onehot, axis=1, keepdims=True), axis=0, keepdims=True))
        hits.append((hit1, hit2, onehot))
    flag_v[...] = jnp.broadcast_to(fullest, flag_v.shape).astype(_I32)
    to_smem = pltpu.make_async_copy(flag_v, flag_s, sem)
    to_smem.start()

    for s in range(N_SUB):
        rows = slice(s * SUB, (s + 1) * SUB)
        hit1, hit2, onehot = hits[s]
        rank = _dot(onehot.astype(_BF16), earlier)
        q1 = (i1[:, rows] - N_GROUPS) * SLOTS + jnp.sum(jnp.where(hit1, rank, 0.0), axis=0, keepdims=True)
        q2 = (i2[:, rows] - N_GROUPS) * SLOTS + jnp.sum(jnp.where(hit2, rank, 0.0), axis=0, keepdims=True)
        at1 = sorted_row == q1
        at2 = sorted_row == q2
        perm = jnp.where(jnp.logical_or(at1, at2), 1.0, 0.0).astype(_BF16)
        perm_ref[s] = perm
        xs_ref[s] = _dot(perm, xb[rows]).astype(_BF16)
        weight = jnp.sum(jnp.where(at1, w1[:, rows], 0.0) + jnp.where(at2, w2[:, rows], 0.0),
                         axis=-1, keepdims=True)
        ws_ref[s] = jnp.broadcast_to(weight, (SUB_SORTED, V7X_LANES))

    to_smem.wait()
    fits = flag_s[0, 0] <= SLOTS

    def finish(rows, moe):
        x2 = _layer_norm(DEEPNORM_ALPHA * x1_ref[rows, :] + moe, g2_ref[...], b2_ref[...])
        gate = jax.nn.sigmoid(_dot(x2.astype(_BF16), wg_ref[...]) + bg_ref[...])
        p = jnp.where(is_sample, ps_ref[0, rows, :], pp_ref[0, rows, :])
        out = x2 + gate * _dot(p.astype(_BF16), wp_ref[...])
        if split_out:
            @pl.when(jnp.logical_not(is_sample))
            def _():
                out_refs[0][rows, :] = out

            @pl.when(is_sample)
            def _():
                out_refs[1][rows, :] = out
        else:
            out_refs[0][rows, :] = out

    @pl.when(fits)
    def _():
        def slots_of(e):
            return slice(e * SLOTS, (e + 1) * SLOTS)

        def up(e):
            xt = jnp.concatenate([xs_ref[s, slots_of(e), :] for s in range(N_SUB)], axis=0)
            return _dot(xt, wgu_ref[e])

        hgu = up(0)
        for e in range(N_EXPERTS):
            hgu_next = up(e + 1) if e + 1 < N_EXPERTS else None
            wt = jnp.concatenate([ws_ref[s, slots_of(e), 0:1] for s in range(N_SUB)], axis=0)
            y = _dot(_expert_act(hgu, wt), wd_ref[e]).astype(_BF16)
            for s in range(N_SUB):
                ys_ref[s, slots_of(e), :] = y[s * SLOTS:(s + 1) * SLOTS, :]
            hgu = hgu_next
        moe = [lax.dot_general(perm_ref[s], ys_ref[s], (((0,), (0,)), ((), ())), preferred_element_type=_F32)
               for s in range(N_SUB)]
        for s in range(N_SUB):
            finish(slice(s * SUB, (s + 1) * SUB), moe[s])

    @pl.when(jnp.logical_not(fits))
    def _():
        row_all = lax.broadcasted_iota(_I32, (V7X_LANES, MOE_TM), 0).astype(_F32)
        combine = (jnp.where(row_all == i1, w1, 0.0) + jnp.where(row_all == i2, w2, 0.0)).T
        acc = jnp.zeros((MOE_TM, D_MODEL), _F32)
        for e in range(N_EXPERTS):
            w = combine[:, N_GROUPS + e:N_GROUPS + e + 1]
            acc = acc + _dot(_expert_act(_dot(xb, wgu_ref[e]), w), wd_ref[e])
        finish(slice(0, MOE_TM), acc)


def _moe_ple(layer, split_out, x1, wr, br, wgu, wd, g2, b2, wg, bg, wp, p_prompt, p_sample):
    prompt_block = lambda i: jnp.minimum(i, N_PROMPT_MOE_BLOCKS - 1)
    sample_block = lambda i: jnp.maximum(i - N_PROMPT_MOE_BLOCKS, 0)
    if split_out:
        out_specs = [pl.BlockSpec((MOE_TM, D_MODEL), lambda i: (prompt_block(i), 0)),
                     pl.BlockSpec((MOE_TM, D_MODEL), lambda i: (sample_block(i), 0))]
        out_shape = [jax.ShapeDtypeStruct((N_PROMPT_TOK, D_MODEL), _F32),
                     jax.ShapeDtypeStruct((N_SAMPLE_TOK, D_MODEL), _F32)]
    else:
        out_specs = [pl.BlockSpec((MOE_TM, D_MODEL), lambda i: (i, 0))]
        out_shape = [jax.ShapeDtypeStruct((N_TOK, D_MODEL), _F32)]
    return pl.pallas_call(
        functools.partial(_moe_ple_kernel, split_out),
        grid=(N_TOK // MOE_TM,),
        in_specs=[
            pl.BlockSpec((MOE_TM, D_MODEL), lambda i: (i, 0)),
            _layer_spec(layer, (ROUTER_ROWS, D_MODEL)),
            _layer_spec(layer, (ROUTER_ROWS, 1)),
            _layer_spec(layer, (N_EXPERTS, D_MODEL, 2 * D_EXPERT)),
            _layer_spec(layer, (N_EXPERTS, D_EXPERT, D_MODEL)),
            _const_spec((1, D_MODEL)),
            _const_spec((1, D_MODEL)),
            _layer_spec(layer, (D_MODEL, D_MODEL)),
            _const_spec((1, D_MODEL)),
            _layer_spec(layer, (PLE_DIM, D_MODEL)),
            pl.BlockSpec((1, MOE_TM, PLE_DIM), lambda i: (layer, prompt_block(i), 0)),
            pl.BlockSpec((1, MOE_TM, PLE_DIM), lambda i: (layer, sample_block(i), 0)),
        ],
        out_specs=out_specs,
        out_shape=out_shape,
        scratch_shapes=[
            pltpu.VMEM((N_SUB, SUB_SORTED, D_MODEL), _BF16),
            pltpu.VMEM((N_SUB, SUB_SORTED, V7X_LANES), _F32),
            pltpu.VMEM((N_SUB, SUB_SORTED, SUB), _BF16),
            pltpu.VMEM((V7X_SUBLANES, V7X_LANES), _I32),
            pltpu.SMEM((V7X_SUBLANES, V7X_LANES), _I32),
            pltpu.SemaphoreType.DMA,
        ],
        compiler_params=_PARAMS,
        name="moe_ple",
    )(x1, wr, br, wgu, wd, g2, b2, wg, bg, wp, p_prompt, p_sample)


def _row(v):
    return v.reshape(1, -1)


def _pad_lanes(a, lanes):
    return jnp.pad(a, [(0, 0)] * (a.ndim - 1) + [(0, lanes - a.shape[-1])])


def _pad_rows(a):
    return jnp.pad(a, [(0, V7X_SUBLANES - a.shape[0]), (0, 0)])


def kernel(x_prompt, x_sample, state_conv, p_prompt, p_sample, a_w_in, a_b_in, a_ln_g, a_ln_b, a_w_s, a_b_s, a_w_out, b_w_in, b_conv_w, b_w_out, ln1_g, ln1_b, ln2_g, ln2_b, moe_w_group, moe_b_group, moe_w_expert, moe_b_expert, moe_w_gate_up, moe_w_down, ple_w_gate, ple_b_gate, ple_w_proj):
    xs = (x_prompt.reshape(N_PROMPT_TOK, D_MODEL), x_sample.reshape(N_SAMPLE_TOK, D_MODEL))
    pp = p_prompt.reshape(DEPTH, N_PROMPT_TOK, PLE_DIM)
    ps = p_sample.reshape(DEPTH, N_SAMPLE_TOK, PLE_DIM)

    reps = CHUNK // DEC_SEQ
    ws_sample = jnp.tile(a_w_s[:, :, :DEC_SEQ, :DEC_SEQ], (1, 1, reps, reps))
    ws2 = jnp.stack([a_w_s, ws_sample], axis=1)
    bs_sample = jnp.tile(a_b_s[:, :, :DEC_SEQ], (1, 1, reps))
    bs2 = _pad_lanes(jnp.swapaxes(jnp.stack([a_b_s, bs_sample], axis=1), -1, -2), V7X_LANES)

    pos = lax.broadcasted_iota(_I32, (1, 1, DEC_SEQ, 1), 2)
    older, newer = state_conv[:, :, 0:1], state_conv[:, :, 1:2]
    past1 = jnp.where(pos == 0, newer, 0.0).reshape(-1, N_SAMPLE_TOK, B_WIDTH)
    past2 = jnp.where(pos == 0, older, jnp.where(pos == 1, newer, 0.0)).reshape(-1, N_SAMPLE_TOK, B_WIDTH)

    w_router = jnp.swapaxes(_pad_lanes(jnp.concatenate([moe_w_group, moe_w_expert], axis=-1), ROUTER_ROWS),
                            -1, -2).astype(_BF16)
    b_router = _pad_lanes(jnp.concatenate([moe_b_group, moe_b_expert], axis=-1), ROUTER_ROWS)[..., None]

    a_w_in, a_w_out, b_w_in, b_w_out, moe_w_gate_up, moe_w_down, ple_w_gate, ple_w_proj = (
        w.astype(_BF16) for w in (a_w_in, a_w_out, b_w_in, b_w_out, moe_w_gate_up, moe_w_down,
                                  ple_w_gate, ple_w_proj))

    v_prompt, v_sample, st_prompt, st_sample = [], [], [], []
    for i in range(DEPTH):
        j = i // 2
        if i % 2 == 0:
            x1, vp, vs = _mixer_a(j, xs, a_w_in, _row(a_b_in[j]), _row(a_ln_g[j]), _row(a_ln_b[j]),
                                  ws2[j], bs2[j], a_w_out, _row(ln1_g[i]), _row(ln1_b[i]))
            v_prompt.append(vp)
            v_sample.append(vs)
        else:
            x1, stp, sts = _mixer_b(j, xs[0], b_w_in, _pad_rows(b_conv_w[j]), b_w_out,
                                    _row(ln1_g[i]), _row(ln1_b[i]), past1[j], past2[j])
            st_prompt.append(stp)
            st_sample.append(sts)
        xs = _moe_ple(i, i == DEPTH - 1, x1, w_router, b_router, moe_w_gate_up, moe_w_down,
                      _row(ln2_g[i]), _row(ln2_b[i]), ple_w_gate, _row(ple_b_gate[i]), ple_w_proj, pp, ps)

    keep = CONV_WIDTH - 1
    return (xs[0].reshape(BATCH, SEQ, D_MODEL),
            xs[1].reshape(DEC_BATCH, DEC_SEQ, D_MODEL),
            jnp.stack(v_prompt).reshape(-1, BATCH, CHUNK, A_WIDTH),
            jnp.stack(v_sample).reshape(-1, DEC_BATCH, DEC_SEQ, A_WIDTH),
            jnp.stack(st_prompt).reshape(-1, BATCH, V7X_SUBLANES, B_WIDTH)[:, :, V7X_SUBLANES - keep:],
            jnp.stack(st_sample).reshape(-1, DEC_BATCH, DEC_SEQ, B_WIDTH)[:, :, DEC_SEQ - keep:])
```
